```python
import math
import jax, jax.numpy as jnp
from jax import lax
import numpy as np

D_MODEL = 1024
BATCH = 32
SEQ = 2048
DEPTH = 2

N_MIXERS = 2
HEAD_DIM = 64
N_HEADS = D_MODEL // HEAD_DIM
Q_BLOCK = 128
CONV_WIDTH = 31
CONV_INNER = D_MODEL
FFN_DIM = 2816
FFN_CONV_WIDTH = 3
EPS = 1e-6
N_A = (DEPTH + 1) // 2
N_B = DEPTH // 2

kernel_name = "fox_conformer_convffn_hybrid"


def rms_norm(x, g):
    xf = x.astype(jnp.float32)
    y = xf * lax.rsqrt(jnp.mean(xf * xf, axis=-1, keepdims=True) + EPS)
    return (y * g.astype(jnp.float32)).astype(x.dtype)


def causal_dwconv(x, w):
    k, c = w.shape
    return lax.conv_general_dilated(
        x, w[:, None, :].astype(x.dtype), window_strides=(1,), padding=[(k - 1, 0)],
        dimension_numbers=("NWC", "WIO", "NWC"), feature_group_count=c)


def fox_attention(h, w_in, b_f, q_g, k_g, w_o):
    b, s, d = h.shape
    proj = h @ w_in
    q = proj[..., :d].reshape(b, s, N_HEADS, HEAD_DIM)
    k = proj[..., d:2 * d].reshape(b, s, N_HEADS, HEAD_DIM)
    v = proj[..., 2 * d:3 * d].reshape(b, s, N_HEADS, HEAD_DIM)
    f_logit = (proj[..., 3 * d:] + b_f).astype(jnp.float32)
    log_f = jax.nn.log_sigmoid(f_logit)
    cum = jnp.transpose(jnp.cumsum(log_f, axis=1), (0, 2, 1))

    q = rms_norm(q, q_g).astype(jnp.float32)
    k = rms_norm(k, k_g).astype(jnp.float32)
    q = jnp.transpose(q, (0, 2, 1, 3))
    k = jnp.transpose(k, (0, 2, 1, 3))
    v = jnp.transpose(v, (0, 2, 1, 3)).astype(jnp.float32)
    scale = 1.0 / math.sqrt(HEAD_DIM)

    outs = []
    for i in range(s // Q_BLOCK):
        q0, q1 = i * Q_BLOCK, (i + 1) * Q_BLOCK
        qb = q[:, :, q0:q1]
        kb, vb = k[:, :, :q1], v[:, :, :q1]
        logits = jnp.einsum("bhqd,bhkd->bhqk", qb, kb) * scale
        logits = logits + (cum[:, :, q0:q1, None] - cum[:, :, None, :q1])
        qpos = jnp.arange(q0, q1)[:, None]
        kpos = jnp.arange(q1)[None, :]
        logits = jnp.where(kpos <= qpos, logits, -jnp.inf)
        p = jax.nn.softmax(logits, axis=-1)
        outs.append(jnp.einsum("bhqk,bhkd->bhqd", p, vb))
    o = jnp.concatenate(outs, axis=2)
    o = jnp.transpose(o, (0, 2, 1, 3)).reshape(b, s, d).astype(h.dtype)
    return o @ w_o


def conformer_conv(h, w_pw1, b_pw1, w_dw, b_dw, ln_g, w_pw2, b_pw2):
    u = h @ w_pw1 + b_pw1
    a, g = jnp.split(u, 2, axis=-1)
    u = a * jax.nn.sigmoid(g)
    u = causal_dwconv(u, w_dw) + b_dw
    u = rms_norm(u, ln_g)
    u = jax.nn.silu(u)
    return u @ w_pw2 + b_pw2


def conv_ffn(h, w_up, w_dw, b_dw, w_down):
    u = h @ w_up
    u = causal_dwconv(u, w_dw) + b_dw
    gate, val = jnp.split(u, 2, axis=-1)
    return (jax.nn.silu(gate) * val) @ w_down


def setup_inputs(seed: int = 0) -> dict:
    key = jax.random.key(seed)
    ks = iter(jax.random.split(key, 32))
    D, H, C, F = D_MODEL, N_HEADS, CONV_INNER, FFN_DIM
    out_scale = (2 * DEPTH) ** -0.5

    def nrm(shape, scale):
        return jax.random.normal(next(ks), shape, jnp.float32) * scale

    def gain(shape):
        return 1.0 + nrm(shape, 0.02)

    x = jax.random.normal(next(ks), (BATCH, SEQ, D), jnp.float32)
    fg_base = jnp.linspace(0.0, 4.0, H, dtype=jnp.float32)
    return {
        "x": x,
        "fox_norm_g": gain((N_A, D)),
        "fox_w_in": nrm((N_A, D, 3 * D + H), D ** -0.5),
        "fox_b_f": fg_base[None, :] + nrm((N_A, H), 0.1),
        "fox_q_g": gain((N_A, H, HEAD_DIM)),
        "fox_k_g": gain((N_A, H, HEAD_DIM)),
        "fox_w_o": nrm((N_A, D, D), D ** -0.5 * out_scale),
        "conv_norm_g": gain((N_B, D)),
        "conv_w_pw1": nrm((N_B, D, 2 * C), D ** -0.5),
        "conv_b_pw1": nrm((N_B, 2 * C), 0.02),
        "conv_w_dw": nrm((N_B, CONV_WIDTH, C), CONV_WIDTH ** -0.5),
        "conv_b_dw": nrm((N_B, C), 0.02),
        "conv_ln_g": gain((N_B, C)),
        "conv_w_pw2": nrm((N_B, C, D), C ** -0.5 * out_scale),
        "conv_b_pw2": nrm((N_B, D), 0.02),
        "ffn_norm_g": gain((DEPTH, D)),
        "ffn_w_up": nrm((DEPTH, D, 2 * F), D ** -0.5),
        "ffn_w_dw": nrm((DEPTH, FFN_CONV_WIDTH, 2 * F), FFN_CONV_WIDTH ** -0.5),
        "ffn_b_dw": nrm((DEPTH, 2 * F), 0.02),
        "ffn_w_down": nrm((DEPTH, F, D), F ** -0.5 * out_scale),
    }


def reference(x, fox_norm_g, fox_w_in, fox_b_f, fox_q_g, fox_k_g, fox_w_o,
              conv_norm_g, conv_w_pw1, conv_b_pw1, conv_w_dw, conv_b_dw, conv_ln_g,
              conv_w_pw2, conv_b_pw2,
              ffn_norm_g, ffn_w_up, ffn_w_dw, ffn_b_dw, ffn_w_down):
    for i in range(DEPTH):
        j = i // N_MIXERS
        if i % N_MIXERS == 0:
            h = rms_norm(x, fox_norm_g[j])
            x = x + fox_attention(h, fox_w_in[j], fox_b_f[j], fox_q_g[j], fox_k_g[j], fox_w_o[j])
        else:
            h = rms_norm(x, conv_norm_g[j])
            x = x + conformer_conv(h, conv_w_pw1[j], conv_b_pw1[j], conv_w_dw[j], conv_b_dw[j],
                                   conv_ln_g[j], conv_w_pw2[j], conv_b_pw2[j])
        h = rms_norm(x, ffn_norm_g[i])
        x = x + conv_ffn(h, ffn_w_up[i], ffn_w_dw[i], ffn_b_dw[i], ffn_w_down[i])
    return x
```

```python
import functools

import jax
import jax.numpy as jnp
from jax import lax
from jax.experimental import pallas as pl
from jax.experimental.pallas import tpu as pltpu

D_MODEL = 1024
HEAD_DIM = 64
N_HEADS = D_MODEL // HEAD_DIM
HEADS_PER_BLOCK = 2
N_HEAD_BLOCKS = N_HEADS // HEADS_PER_BLOCK
FFN_DIM = 2816
FFN_TAPS = 3
CONV_TAPS = 31
EPS = 1e-6

LANES = 128
SUBLANES = 8
MXU_DIM = 256
VMEM_LIMIT_BYTES = 56 * 1024 * 1024

ROW_TILE = 1024
ATTN_TILE = 256
FFN_CHUNK = 256
N_FFN_CHUNKS = FFN_DIM // FFN_CHUNK
CONV_HALO = 32
FFN_HALO = 8
CONV_ROW_BLOCK = 64
GATE_COLS = LANES
MASK_VALUE = -1e30

_F32 = jnp.float32
_BF16 = jnp.bfloat16


def _rms_rows(x, gain):
    ms = jnp.mean(x * x, axis=-1, keepdims=True)
    return x * lax.rsqrt(ms + EPS) * gain


def _sigmoid(x):
    return 1.0 / (1.0 + jnp.exp(-x))


def _log_sigmoid(x):
    return jnp.minimum(x, 0.0) - jnp.log1p(jnp.exp(-jnp.abs(x)))


def _compiler_params(semantics):
    return pltpu.CompilerParams(dimension_semantics=semantics,
                                vmem_limit_bytes=VMEM_LIMIT_BYTES)


def _resident():
    return pl.BlockSpec(memory_space=pltpu.VMEM)


def _fox_inproj_kernel(x_ref, g_ref, w_ref, bf_ref, qg_ref, kg_ref, bd_ref,
                       qkv_ref, cum_ref, cumt_ref, h_ref, carry_ref, *, tiles_per_seq):
    i = pl.program_id(0)
    tm = x_ref.shape[0]
    h_ref[...] = _rms_rows(x_ref[...], g_ref[...]).astype(_BF16)

    n_chunk = 2 * MXU_DIM
    for c in range(3 * D_MODEL // n_chunk):
        cols = slice(c * n_chunk, (c + 1) * n_chunk)
        acc = jnp.dot(h_ref[...], w_ref[:, cols], preferred_element_type=_F32)
        if c * n_chunk < 2 * D_MODEL:
            is_q = c * n_chunk < D_MODEL
            gain_ref = qg_ref if is_q else kg_ref
            gcols = slice((c * n_chunk) % D_MODEL, (c * n_chunk) % D_MODEL + n_chunk)
            sq = (acc * acc).astype(_BF16)
            ssq = jnp.concatenate(
                [jnp.dot(sq[:, s * MXU_DIM:(s + 1) * MXU_DIM], bd_ref[...],
                         preferred_element_type=_F32) for s in range(n_chunk // MXU_DIM)],
                axis=1)
            acc = acc * lax.rsqrt(ssq * (1.0 / HEAD_DIM) + EPS) * gain_ref[:, gcols]
            if is_q:
                acc = acc * (HEAD_DIM ** -0.5)
        qkv_ref[:, cols] = acc.astype(_BF16)

    logit = jnp.dot(h_ref[...], w_ref[:, 3 * D_MODEL:3 * D_MODEL + GATE_COLS],
                    preferred_element_type=_F32) + bf_ref[...]
    c = _log_sigmoid(logit)
    row = lax.broadcasted_iota(jnp.int32, c.shape, 0)
    d = 1
    while d < tm:
        c = c + jnp.where(row >= d, pltpu.roll(c, d, axis=0), 0.0)
        d *= 2

    @pl.when(i % tiles_per_seq == 0)
    def _():
        carry_ref[...] = jnp.zeros_like(carry_ref)

    c = c + carry_ref[0:1, :]
    carry_ref[...] = jnp.broadcast_to(c[tm - 1:tm, :], carry_ref.shape)
    cum_ref[...] = c
    ct = c.T
    for hb in range(N_HEAD_BLOCKS):
        cumt_ref[0, hb] = ct[hb * HEADS_PER_BLOCK:(hb + 1) * HEADS_PER_BLOCK, :]


def _fox_inproj(x2d, g, w, bf, qg, kg, bd, *, seq):
    n = x2d.shape[0]
    tm = ROW_TILE
    tps = seq // tm
    batch = n // seq
    wn = w.shape[1]
    return pl.pallas_call(
        functools.partial(_fox_inproj_kernel, tiles_per_seq=tps),
        grid=(n // tm,),
        in_specs=[
            pl.BlockSpec((tm, D_MODEL), lambda i: (i, 0)),
            _resident(), _resident(), _resident(), _resident(), _resident(), _resident(),
        ],
        out_specs=[
            pl.BlockSpec((tm, 3 * D_MODEL), lambda i: (i, 0)),
            pl.BlockSpec((tm, LANES), lambda i: (i, 0)),
            pl.BlockSpec((1, N_HEAD_BLOCKS, HEADS_PER_BLOCK, tm),
                         lambda i: (i // tps, 0, 0, i % tps)),
        ],
        out_shape=[
            jax.ShapeDtypeStruct((n, 3 * D_MODEL), _BF16),
            jax.ShapeDtypeStruct((n, LANES), _F32),
            jax.ShapeDtypeStruct((batch, N_HEAD_BLOCKS, HEADS_PER_BLOCK, seq), _F32),
        ],
        scratch_shapes=[pltpu.VMEM((tm, D_MODEL), _BF16),
                        pltpu.VMEM((SUBLANES, LANES), _F32)],
        compiler_params=_compiler_params(("arbitrary",)),
        name="fox_inproj",
    )(x2d, g, w, bf, qg, kg, bd)


def _fox_attn_kernel(q_ref, k_ref, v_ref, cum_ref, cumt_ref, o_ref):
    hb = pl.program_id(1)
    i = pl.program_id(2)
    tq = q_ref.shape[1]
    tk = tq
    q = q_ref[0]
    cum = cum_ref[0]
    lane = lax.broadcasted_iota(jnp.int32, (tq, LANES), 1)
    rows = lax.broadcasted_iota(jnp.int32, (tq, tk), 0)
    cols = lax.broadcasted_iota(jnp.int32, (tq, tk), 1)

    outs = []
    for e in range(HEADS_PER_BLOCK):
        head = hb * HEADS_PER_BLOCK + e
        in_head = (lane >= e * HEAD_DIM) & (lane < (e + 1) * HEAD_DIM)
        qe = jnp.where(in_head, q, jnp.zeros_like(q))
        cq = jnp.sum(jnp.where(lane == head, cum, 0.0), axis=1, keepdims=True)

        def step(j, carry, masked, qe=qe, cq=cq, e=e):
            m, l, acc = carry
            k0 = pl.multiple_of(j * tk, tk)
            kt = k_ref[0, pl.ds(k0, tk), :]
            vt = v_ref[0, pl.ds(k0, tk), :]
            s = lax.dot_general(qe, kt, (((1,), (1,)), ((), ())),
                                preferred_element_type=_F32)
            s = s - cumt_ref[0, 0, e:e + 1, pl.ds(k0, tk)]
            if masked:
                s = jnp.where(cols <= rows, s, MASK_VALUE)
            m_new = jnp.maximum(m, jnp.max(s, axis=1, keepdims=True) + cq)
            alpha = jnp.exp(m - m_new)
            p = jnp.exp(s + (cq - m_new))
            l = alpha * l + jnp.sum(p, axis=1, keepdims=True)
            acc = alpha * acc + jnp.dot(p.astype(_BF16), vt, preferred_element_type=_F32)
            return m_new, l, acc

        init = (jnp.full((tq, 1), MASK_VALUE, _F32), jnp.zeros((tq, 1), _F32),
                jnp.zeros((tq, LANES), _F32))
        carry = lax.fori_loop(0, i, functools.partial(step, masked=False), init)
        _, l, acc = step(i, carry, masked=True)
        outs.append(acc / l)

    o_ref[0] = jnp.where(lane < HEAD_DIM, outs[0], outs[1]).astype(o_ref.dtype)


def _fox_attn(qkv, cum, cumt):
    batch, seq, _ = qkv.shape
    tq = ATTN_TILE
    return pl.pallas_call(
        _fox_attn_kernel,
        grid=(batch, N_HEAD_BLOCKS, seq // tq),
        in_specs=[
            pl.BlockSpec((1, tq, LANES), lambda b, h, i: (b, i, h)),
            pl.BlockSpec((1, seq, LANES), lambda b, h, i: (b, 0, N_HEAD_BLOCKS + h)),
            pl.BlockSpec((1, seq, LANES), lambda b, h, i: (b, 0, 2 * N_HEAD_BLOCKS + h)),
            pl.BlockSpec((1, tq, LANES), lambda b, h, i: (b, i, 0)),
            pl.BlockSpec((1, 1, HEADS_PER_BLOCK, seq), lambda b, h, i: (b, h, 0, 0)),
        ],
        out_specs=pl.BlockSpec((1, tq, LANES), lambda b, h, i: (b, i, h)),
        out_shape=jax.ShapeDtypeStruct((batch, seq, D_MODEL), _BF16),
        compiler_params=_compiler_params(("arbitrary", "arbitrary", "arbitrary")),
        name="fox_attn",
    )(qkv, qkv, qkv, cum, cumt)


def _outproj_kernel(x_ref, o_ref, w_ref, y_ref):
    y_ref[...] = x_ref[...] + jnp.dot(o_ref[...], w_ref[...], preferred_element_type=_F32)


def _outproj(x2d, o2d, w):
    n = x2d.shape[0]
    tm = ROW_TILE
    return pl.pallas_call(
        _outproj_kernel,
        grid=(n // tm,),
        in_specs=[pl.BlockSpec((tm, D_MODEL), lambda i: (i, 0)),
                  pl.BlockSpec((tm, D_MODEL), lambda i: (i, 0)),
                  _resident()],
        out_specs=pl.BlockSpec((tm, D_MODEL), lambda i: (i, 0)),
        out_shape=jax.ShapeDtypeStruct((n, D_MODEL), _F32),
        compiler_params=_compiler_params(("arbitrary",)),
        name="fox_outproj",
    )(x2d, o2d, w)


def _ffn_kernel(x_ref, xprev_ref, g_ref, wup_ref, wdw_ref, wdown_ref, y_ref,
                h_ref, u_ref, acc_ref, *, tiles_per_seq):
    i = pl.program_id(0)
    tm = x_ref.shape[0]
    halo = FFN_HALO
    h_ref[halo:, :] = _rms_rows(x_ref[...], g_ref[...]).astype(_BF16)
    hprev = _rms_rows(xprev_ref[...], g_ref[...])
    hprev = jnp.where(i % tiles_per_seq == 0, 0.0, hprev)
    h_ref[:halo, :] = hprev.astype(_BF16)
    acc_ref[...] = x_ref[...]

    def chunk(c, carry):
        u_ref[...] = jnp.dot(h_ref[...], wup_ref[c], preferred_element_type=_F32)
        wd = wdw_ref[c]
        y = wd[FFN_TAPS:FFN_TAPS + 1, :]
        for k in range(FFN_TAPS):
            off = halo - (FFN_TAPS - 1) + k
            y = y + wd[k:k + 1, :] * u_ref[off:off + tm, :]
        gate = y[:, :FFN_CHUNK]
        val = y[:, FFN_CHUNK:]
        a = (gate * _sigmoid(gate)) * val
        acc_ref[...] += jnp.dot(a.astype(_BF16), wdown_ref[c], preferred_element_type=_F32)
        return carry

    lax.fori_loop(0, N_FFN_CHUNKS, chunk, 0)
    y_ref[...] = acc_ref[...]


def _ffn(x2d, g, wup, wdw, wdown, *, seq):
    n = x2d.shape[0]
    tm = ROW_TILE
    tps = seq // tm
    hblocks = tm // FFN_HALO
    return pl.pallas_call(
        functools.partial(_ffn_kernel, tiles_per_seq=tps),
        grid=(n // tm,),
        in_specs=[
            pl.BlockSpec((tm, D_MODEL), lambda i: (i, 0)),
            pl.BlockSpec((FFN_HALO, D_MODEL), lambda i: (jnp.maximum(i * hblocks - 1, 0), 0)),
            _resident(), _resident(), _resident(), _resident(),
        ],
        out_specs=pl.BlockSpec((tm, D_MODEL), lambda i: (i, 0)),
        out_shape=jax.ShapeDtypeStruct((n, D_MODEL), _F32),
        scratch_shapes=[pltpu.VMEM((tm + FFN_HALO, D_MODEL), _BF16),
                        pltpu.VMEM((tm + FFN_HALO, 2 * FFN_CHUNK), _F32),
                        pltpu.VMEM((tm, D_MODEL), _F32)],
        compiler_params=_compiler_params(("arbitrary",)),
        name="conv_ffn",
    )(x2d, x2d, g, wup, wdw, wdown)


def _conformer_kernel(x_ref, xprev_ref, g_ref, w1_ref, b1_ref, wdw_ref, lng_ref, w2_ref, b2_ref,
                      y_ref, h_ref, glu_ref, conv_ref, shift_ref, *, tiles_per_seq):
    i = pl.program_id(0)
    tm = x_ref.shape[0]
    halo = CONV_HALO
    c_inner = glu_ref.shape[1]
    h_ref[halo:, :] = _rms_rows(x_ref[...], g_ref[...]).astype(_BF16)
    h_ref[:halo, :] = _rms_rows(xprev_ref[...], g_ref[...]).astype(_BF16)

    for cb in range(c_inner // MXU_DIM):
        ca = slice(cb * MXU_DIM, (cb + 1) * MXU_DIM)
        cg = slice(c_inner + cb * MXU_DIM, c_inner + (cb + 1) * MXU_DIM)
        a = jnp.dot(h_ref[...], w1_ref[:, ca], preferred_element_type=_F32) + b1_ref[:, ca]
        g = jnp.dot(h_ref[...], w1_ref[:, cg], preferred_element_type=_F32) + b1_ref[:, cg]
        glu_ref[:, ca] = a * _sigmoid(g)

    @pl.when(i % tiles_per_seq == 0)
    def _():
        glu_ref[:halo, :] = jnp.zeros((halo, c_inner), _F32)

    rb = CONV_ROW_BLOCK
    first = halo - (CONV_TAPS - 1)
    n_sh = shift_ref.shape[1]
    for lt in range(c_inner // LANES):
        ln = slice(lt * LANES, (lt + 1) * LANES)
        for r in range(1, SUBLANES):
            shift_ref[r - 1] = glu_ref[r:r + n_sh, ln]

        def conv_rows(blk, carry, ln=ln):
            base = pl.multiple_of(blk * rb, rb)
            acc = jnp.broadcast_to(wdw_ref[CONV_TAPS:CONV_TAPS + 1, ln], (rb, LANES))
            for k in range(CONV_TAPS):
                a, r = divmod(first + k, SUBLANES)
                rows = pl.ds(base + a * SUBLANES, rb)
                src = glu_ref[rows, ln] if r == 0 else shift_ref[r - 1, rows, :]
                acc = acc + wdw_ref[k:k + 1, ln] * src
            conv_ref[pl.ds(base, rb), ln] = acc
            return carry

        lax.fori_loop(0, tm // rb, conv_rows, 0)

    u = _rms_rows(conv_ref[...], lng_ref[...])
    u = u * _sigmoid(u)
    y_ref[...] = (x_ref[...] + b2_ref[...]
                  + jnp.dot(u.astype(_BF16), w2_ref[...], preferred_element_type=_F32))


def _conformer(x2d, g, w1, b1, wdw, lng, w2, b2, *, seq):
    n = x2d.shape[0]
    tm = ROW_TILE
    tps = seq // tm
    hblocks = tm // CONV_HALO
    c_inner = w2.shape[0]
    return pl.pallas_call(
        functools.partial(_conformer_kernel, tiles_per_seq=tps),
        grid=(n // tm,),
        in_specs=[
            pl.BlockSpec((tm, D_MODEL), lambda i: (i, 0)),
            pl.BlockSpec((CONV_HALO, D_MODEL), lambda i: (jnp.maximum(i * hblocks - 1, 0), 0)),
            _resident(), _resident(), _resident(), _resident(), _resident(), _resident(),
            _resident(),
        ],
        out_specs=pl.BlockSpec((tm, D_MODEL), lambda i: (i, 0)),
        out_shape=jax.ShapeDtypeStruct((n, D_MODEL), _F32),
        scratch_shapes=[pltpu.VMEM((tm + CONV_HALO, D_MODEL), _BF16),
                        pltpu.VMEM((tm + CONV_HALO, c_inner), _F32),
                        pltpu.VMEM((tm, c_inner), _F32),
                        pltpu.VMEM((SUBLANES - 1, tm + CONV_HALO - SUBLANES, LANES), _F32)],
        compiler_params=_compiler_params(("arbitrary",)),
        name="conformer_conv",
    )(x2d, x2d, g, w1, b1, wdw, lng, w2, b2)


def _prep_fox(w_in, b_f, q_g, k_g):
    pad = GATE_COLS - N_HEADS
    w = jnp.pad(w_in, ((0, 0), (0, pad))).astype(_BF16)
    bf = jnp.pad(b_f, (0, pad)).reshape(1, GATE_COLS)
    head = jnp.arange(MXU_DIM) // HEAD_DIM
    bd = (head[:, None] == head[None, :]).astype(_BF16)
    return w, bf, q_g.reshape(1, D_MODEL), k_g.reshape(1, D_MODEL), bd


def _prep_ffn(w_up, w_dw, b_dw, w_down):
    def regroup(a):
        lead = a.shape[:-1]
        a = a.reshape(lead + (2, N_FFN_CHUNKS, FFN_CHUNK))
        a = jnp.moveaxis(a, -2, 0)
        return a.reshape((N_FFN_CHUNKS,) + lead + (2 * FFN_CHUNK,))

    wup = regroup(w_up).astype(_BF16)
    taps = regroup(w_dw)
    bias = regroup(b_dw[None, :])
    fill = jnp.zeros((N_FFN_CHUNKS, SUBLANES - FFN_TAPS - 1, 2 * FFN_CHUNK), _F32)
    wdw = jnp.concatenate([taps, bias, fill], axis=1)
    wdown = w_down.reshape(N_FFN_CHUNKS, FFN_CHUNK, D_MODEL).astype(_BF16)
    return wup, wdw, wdown


def kernel(x, fox_norm_g, fox_w_in, fox_b_f, fox_q_g, fox_k_g, fox_w_o, conv_norm_g, conv_w_pw1,
           conv_b_pw1, conv_w_dw, conv_b_dw, conv_ln_g, conv_w_pw2, conv_b_pw2, ffn_norm_g,
           ffn_w_up, ffn_w_dw, ffn_b_dw, ffn_w_down):
    batch, seq, d = x.shape
    depth = ffn_w_up.shape[0]
    x2d = x.reshape(batch * seq, d)
    for layer in range(depth):
        j = layer // 2
        if layer % 2 == 0:
            w, bf, qg, kg, bd = _prep_fox(fox_w_in[j], fox_b_f[j], fox_q_g[j], fox_k_g[j])
            qkv, cum, cumt = _fox_inproj(x2d, fox_norm_g[j].reshape(1, d), w, bf, qg, kg, bd,
                                         seq=seq)
            o = _fox_attn(qkv.reshape(batch, seq, 3 * d), cum.reshape(batch, seq, LANES), cumt)
            x2d = _outproj(x2d, o.reshape(batch * seq, d), fox_w_o[j].astype(_BF16))
        else:
            wdw = jnp.concatenate([conv_w_dw[j], conv_b_dw[j][None, :]], axis=0)
            x2d = _conformer(x2d, conv_norm_g[j].reshape(1, d), conv_w_pw1[j].astype(_BF16),
                             conv_b_pw1[j].reshape(1, -1), wdw, conv_ln_g[j].reshape(1, -1),
                             conv_w_pw2[j].astype(_BF16), conv_b_pw2[j].reshape(1, d), seq=seq)
        wup, wdw, wdown = _prep_ffn(ffn_w_up[layer], ffn_w_dw[layer], ffn_b_dw[layer],
                                    ffn_w_down[layer])
        x2d = _ffn(x2d, ffn_norm_g[layer].reshape(1, d), wup, wdw, wdown, seq=seq)
    return x2d.reshape(batch, seq, d)
```

```python
import functools

import jax
import jax.numpy as jnp
from jax import lax
from jax.experimental import pallas as pl
from jax.experimental.pallas import tpu as pltpu

D_MODEL = 1024
HEAD_DIM = 64
N_HEADS = D_MODEL // HEAD_DIM
HEADS_PER_BLOCK = 2
N_HEAD_BLOCKS = N_HEADS // HEADS_PER_BLOCK
FFN_DIM = 2816
FFN_TAPS = 3
CONV_TAPS = 31
EPS = 1e-6

LANES = 128
SUBLANES = 8
MXU_DIM = 256
VMEM_LIMIT_BYTES = 56 * 1024 * 1024

ROW_TILE = 1024
ATTN_TILE = 256
FFN_CHUNK = 256
N_FFN_CHUNKS = FFN_DIM // FFN_CHUNK
CONV_HALO = 32
FFN_HALO = 8
CONV_ROW_BLOCK = 64
GATE_COLS = LANES
MASK_VALUE = -1e30

_F32 = jnp.float32
_BF16 = jnp.bfloat16


def _rms_rows(x, gain):
    ms = jnp.mean(x * x, axis=-1, keepdims=True)
    return x * lax.rsqrt(ms + EPS) * gain


def _sigmoid(x):
    return 1.0 / (1.0 + jnp.exp(-x))


def _log_sigmoid(x):
    return jnp.minimum(x, 0.0) - jnp.log1p(jnp.exp(-jnp.abs(x)))


def _compiler_params(semantics):
    return pltpu.CompilerParams(dimension_semantics=semantics,
                                vmem_limit_bytes=VMEM_LIMIT_BYTES)


def _resident():
    return pl.BlockSpec(memory_space=pltpu.VMEM)


def _fox_inproj_kernel(x_ref, g_ref, w_ref, bf_ref, qg_ref, kg_ref, bd_ref,
                       qkv_ref, cum_ref, cumt_ref, h_ref, carry_ref, *, tiles_per_seq):
    i = pl.program_id(0)
    tm = x_ref.shape[0]
    h_ref[...] = _rms_rows(x_ref[...], g_ref[...]).astype(_BF16)

    n_chunk = 2 * MXU_DIM
    for c in range(3 * D_MODEL // n_chunk):
        cols = slice(c * n_chunk, (c + 1) * n_chunk)
        acc = jnp.dot(h_ref[...], w_ref[:, cols], preferred_element_type=_F32)
        if c * n_chunk < 2 * D_MODEL:
            is_q = c * n_chunk < D_MODEL
            gain_ref = qg_ref if is_q else kg_ref
            gcols = slice((c * n_chunk) % D_MODEL, (c * n_chunk) % D_MODEL + n_chunk)
            sq = (acc * acc).astype(_BF16)
            ssq = jnp.concatenate(
                [jnp.dot(sq[:, s * MXU_DIM:(s + 1) * MXU_DIM], bd_ref[...],
                         preferred_element_type=_F32) for s in range(n_chunk // MXU_DIM)],
                axis=1)
            acc = acc * lax.rsqrt(ssq * (1.0 / HEAD_DIM) + EPS) * gain_ref[:, gcols]
            if is_q:
                acc = acc * (HEAD_DIM ** -0.5)
        qkv_ref[:, cols] = acc.astype(_BF16)

    logit = jnp.dot(h_ref[...], w_ref[:, 3 * D_MODEL:3 * D_MODEL + GATE_COLS],
                    preferred_element_type=_F32) + bf_ref[...]
    c = _log_sigmoid(logit)
    row = lax.broadcasted_iota(jnp.int32, c.shape, 0)
    d = 1
    while d < tm:
        c = c + jnp.where(row >= d, pltpu.roll(c, d, axis=0), 0.0)
        d *= 2

    @pl.when(i % tiles_per_seq == 0)
    def _():
        carry_ref[...] = jnp.zeros_like(carry_ref)

    c = c + carry_ref[0:1, :]
    carry_ref[...] = jnp.broadcast_to(c[tm - 1:tm, :], carry_ref.shape)
    cum_ref[...] = c
    ct = c.T
    for hb in range(N_HEAD_BLOCKS):
        cumt_ref[0, hb] = ct[hb * HEADS_PER_BLOCK:(hb + 1) * HEADS_PER_BLOCK, :]


def _fox_inproj(x2d, g, w, bf, qg, kg, bd, *, seq):
    n = x2d.shape[0]
    tm = ROW_TILE
    tps = seq // tm
    batch = n // seq
    wn = w.shape[1]
    return pl.pallas_call(
        functools.partial(_fox_inproj_kernel, tiles_per_seq=tps),
        grid=(n // tm,),
        in_specs=[
            pl.BlockSpec((tm, D_MODEL), lambda i: (i, 0)),
            _resident(), _resident(), _resident(), _resident(), _resident(), _resident(),
        ],
        out_specs=[
            pl.BlockSpec((tm, 3 * D_MODEL), lambda i: (i, 0)),
            pl.BlockSpec((tm, LANES), lambda i: (i, 0)),
            pl.BlockSpec((1, N_HEAD_BLOCKS, HEADS_PER_BLOCK, tm),
                         lambda i: (i // tps, 0, 0, i % tps)),
        ],
        out_shape=[
            jax.ShapeDtypeStruct((n, 3 * D_MODEL), _BF16),
            jax.ShapeDtypeStruct((n, LANES), _F32),
            jax.ShapeDtypeStruct((batch, N_HEAD_BLOCKS, HEADS_PER_BLOCK, seq), _F32),
        ],
        scratch_shapes=[pltpu.VMEM((tm, D_MODEL), _BF16),
                        pltpu.VMEM((SUBLANES, LANES), _F32)],
        compiler_params=_compiler_params(("arbitrary",)),
        name="fox_inproj",
    )(x2d, g, w, bf, qg, kg, bd)


def _fox_attn_kernel(q_ref, k_ref, v_ref, cum_ref, cumt_ref, o_ref, qs_ref, s_ref, p_ref):
    hb = pl.program_id(1)
    seq = q_ref.shape[1]
    tq = ATTN_TILE
    tk = ATTN_TILE
    lane = lax.broadcasted_iota(jnp.int32, (tq, LANES), 1)
    causal = (lax.broadcasted_iota(jnp.int32, (tq, tk), 1)
              <= lax.broadcasted_iota(jnp.int32, (tq, tk), 0))
    heads = range(HEADS_PER_BLOCK)

    def fold(a):
        return [a[:, b * LANES:(b + 1) * LANES] for b in range(tk // LANES)]

    for i in range(seq // tq):
        qrows = slice(i * tq, (i + 1) * tq)
        q = q_ref[0, qrows, :]
        cum = cum_ref[0, qrows, :]
        cq = []
        for e in heads:
            in_head = (lane >= e * HEAD_DIM) & (lane < (e + 1) * HEAD_DIM)
            qs_ref[e * tq:(e + 1) * tq, :] = jnp.where(in_head, q, jnp.zeros_like(q))
            head = hb * HEADS_PER_BLOCK + e
            cq.append(jnp.sum(jnp.where(lane == head, cum, 0.0), axis=1, keepdims=True))

        mx = [jnp.full((tq, LANES), MASK_VALUE, _F32) for _ in heads]
        for j in range(i + 1):
            kcols = slice(j * tk, (j + 1) * tk)
            s = lax.dot_general(qs_ref[...], k_ref[0, kcols, :], (((1,), (1,)), ((), ())),
                                preferred_element_type=_F32)
            for e in heads:
                se = s[e * tq:(e + 1) * tq, :] - cumt_ref[0, 0, e:e + 1, kcols]
                if j == i:
                    se = jnp.where(causal, se, MASK_VALUE)
                s_ref[e * tq:(e + 1) * tq, kcols] = se
                for part in fold(se):
                    mx[e] = jnp.maximum(mx[e], part)

        shift = []
        for e in heads:
            m = jnp.max(mx[e], axis=1, keepdims=True) + cq[e]
            shift.append(cq[e] - m)
        ls = [jnp.zeros((tq, LANES), _F32) for _ in heads]
        for j in range(i + 1):
            kcols = slice(j * tk, (j + 1) * tk)
            for e in heads:
                erows = slice(e * tq, (e + 1) * tq)
                p = jnp.exp(s_ref[erows, kcols] + shift[e])
                for part in fold(p):
                    ls[e] = ls[e] + part
                p_ref[erows, kcols] = p.astype(_BF16)

        q1 = (i + 1) * tq
        pv = jnp.dot(p_ref[:, :q1], v_ref[0, :q1, :], preferred_element_type=_F32)
        outs = [pv[e * tq:(e + 1) * tq, :] / jnp.sum(ls[e], axis=1, keepdims=True) for e in heads]
        o_ref[0, qrows, :] = jnp.where(lane < HEAD_DIM, outs[0], outs[1]).astype(o_ref.dtype)


def _fox_attn(qkv, cum, cumt):
    batch, seq, _ = qkv.shape
    rows = HEADS_PER_BLOCK * ATTN_TILE
    return pl.pallas_call(
        _fox_attn_kernel,
        grid=(batch, N_HEAD_BLOCKS),
        in_specs=[
            pl.BlockSpec((1, seq, LANES), lambda b, h: (b, 0, h)),
            pl.BlockSpec((1, seq, LANES), lambda b, h: (b, 0, N_HEAD_BLOCKS + h)),
            pl.BlockSpec((1, seq, LANES), lambda b, h: (b, 0, 2 * N_HEAD_BLOCKS + h)),
            pl.BlockSpec((1, seq, LANES), lambda b, h: (b, 0, 0)),
            pl.BlockSpec((1, 1, HEADS_PER_BLOCK, seq), lambda b, h: (b, h, 0, 0)),
        ],
        out_specs=pl.BlockSpec((1, seq, LANES), lambda b, h: (b, 0, h)),
        out_shape=jax.ShapeDtypeStruct((batch, seq, D_MODEL), _BF16),
        scratch_shapes=[pltpu.VMEM((rows, LANES), _BF16),
                        pltpu.VMEM((rows, seq), _F32),
                        pltpu.VMEM((rows, seq), _BF16)],
        compiler_params=_compiler_params(("arbitrary", "arbitrary")),
        name="fox_attn",
    )(qkv, qkv, qkv, cum, cumt)


def _outproj_kernel(x_ref, o_ref, w_ref, y_ref):
    y_ref[...] = x_ref[...] + jnp.dot(o_ref[...], w_ref[...], preferred_element_type=_F32)


def _outproj(x2d, o2d, w):
    n = x2d.shape[0]
    tm = ROW_TILE
    return pl.pallas_call(
        _outproj_kernel,
        grid=(n // tm,),
        in_specs=[pl.BlockSpec((tm, D_MODEL), lambda i: (i, 0)),
                  pl.BlockSpec((tm, D_MODEL), lambda i: (i, 0)),
                  _resident()],
        out_specs=pl.BlockSpec((tm, D_MODEL), lambda i: (i, 0)),
        out_shape=jax.ShapeDtypeStruct((n, D_MODEL), _F32),
        compiler_params=_compiler_params(("arbitrary",)),
        name="fox_outproj",
    )(x2d, o2d, w)


def _ffn_kernel(x_ref, xprev_ref, g_ref, wup_ref, wdw_ref, wdown_ref, y_ref,
                h_ref, u_ref, acc_ref, *, tiles_per_seq):
    i = pl.program_id(0)
    tm = x_ref.shape[0]
    halo = FFN_HALO
    h_ref[halo:, :] = _rms_rows(x_ref[...], g_ref[...]).astype(_BF16)
    hprev = _rms_rows(xprev_ref[...], g_ref[...])
    hprev = jnp.where(i % tiles_per_seq == 0, 0.0, hprev)
    h_ref[:halo, :] = hprev.astype(_BF16)
    acc_ref[...] = x_ref[...]

    def up(c):
        u_ref[c % 2] = jnp.dot(h_ref[...], wup_ref[c], preferred_element_type=_F32)

    up(0)
    for c in range(N_FFN_CHUNKS):
        if c + 1 < N_FFN_CHUNKS:
            up(c + 1)
        wd = wdw_ref[c]
        y = wd[FFN_TAPS:FFN_TAPS + 1, :]
        for k in range(FFN_TAPS):
            off = halo - (FFN_TAPS - 1) + k
            y = y + wd[k:k + 1, :] * u_ref[c % 2, off:off + tm, :]
        gate = y[:, :FFN_CHUNK]
        val = y[:, FFN_CHUNK:]
        a = (gate * _sigmoid(gate)) * val
        acc_ref[...] += jnp.dot(a.astype(_BF16), wdown_ref[c], preferred_element_type=_F32)
    y_ref[...] = acc_ref[...]


def _ffn(x2d, g, wup, wdw, wdown, *, seq):
    n = x2d.shape[0]
    tm = ROW_TILE
    tps = seq // tm
    hblocks = tm // FFN_HALO
    return pl.pallas_call(
        functools.partial(_ffn_kernel, tiles_per_seq=tps),
        grid=(n // tm,),
        in_specs=[
            pl.BlockSpec((tm, D_MODEL), lambda i: (i, 0)),
            pl.BlockSpec((FFN_HALO, D_MODEL), lambda i: (jnp.maximum(i * hblocks - 1, 0), 0)),
            _resident(), _resident(), _resident(), _resident(),
        ],
        out_specs=pl.BlockSpec((tm, D_MODEL), lambda i: (i, 0)),
        out_shape=jax.ShapeDtypeStruct((n, D_MODEL), _F32),
        scratch_shapes=[pltpu.VMEM((tm + FFN_HALO, D_MODEL), _BF16),
                        pltpu.VMEM((2, tm + FFN_HALO, 2 * FFN_CHUNK), _F32),
                        pltpu.VMEM((tm, D_MODEL), _F32)],
        compiler_params=_compiler_params(("arbitrary",)),
        name="conv_ffn",
    )(x2d, x2d, g, wup, wdw, wdown)


def _conformer_kernel(x_ref, xprev_ref, g_ref, w1_ref, b1_ref, wdw_ref, lng_ref, w2_ref, b2_ref,
                      y_ref, h_ref, glu_ref, conv_ref, shift_ref, *, tiles_per_seq):
    i = pl.program_id(0)
    tm = x_ref.shape[0]
    halo = CONV_HALO
    c_inner = glu_ref.shape[1]
    h_ref[halo:, :] = _rms_rows(x_ref[...], g_ref[...]).astype(_BF16)
    h_ref[:halo, :] = _rms_rows(xprev_ref[...], g_ref[...]).astype(_BF16)

    for cb in range(c_inner // MXU_DIM):
        ca = slice(cb * MXU_DIM, (cb + 1) * MXU_DIM)
        cg = slice(c_inner + cb * MXU_DIM, c_inner + (cb + 1) * MXU_DIM)
        a = jnp.dot(h_ref[...], w1_ref[:, ca], preferred_element_type=_F32) + b1_ref[:, ca]
        g = jnp.dot(h_ref[...], w1_ref[:, cg], preferred_element_type=_F32) + b1_ref[:, cg]
        glu_ref[:, ca] = a * _sigmoid(g)

    @pl.when(i % tiles_per_seq == 0)
    def _():
        glu_ref[:halo, :] = jnp.zeros((halo, c_inner), _F32)

    rb = CONV_ROW_BLOCK
    first = halo - (CONV_TAPS - 1)
    n_sh = shift_ref.shape[1]
    for lt in range(c_inner // LANES):
        ln = slice(lt * LANES, (lt + 1) * LANES)
        for r in range(1, SUBLANES):
            shift_ref[r - 1] = glu_ref[r:r + n_sh, ln]

        def conv_rows(blk, carry, ln=ln):
            base = pl.multiple_of(blk * rb, rb)
            acc = jnp.broadcast_to(wdw_ref[CONV_TAPS:CONV_TAPS + 1, ln], (rb, LANES))
            for k in range(CONV_TAPS):
                a, r = divmod(first + k, SUBLANES)
                rows = pl.ds(base + a * SUBLANES, rb)
                src = glu_ref[rows, ln] if r == 0 else shift_ref[r - 1, rows, :]
                acc = acc + wdw_ref[k:k + 1, ln] * src
            conv_ref[pl.ds(base, rb), ln] = acc
            return carry

        lax.fori_loop(0, tm // rb, conv_rows, 0)

    u = _rms_rows(conv_ref[...], lng_ref[...])
    u = u * _sigmoid(u)
    y_ref[...] = (x_ref[...] + b2_ref[...]
                  + jnp.dot(u.astype(_BF16), w2_ref[...], preferred_element_type=_F32))


def _conformer(x2d, g, w1, b1, wdw, lng, w2, b2, *, seq):
    n = x2d.shape[0]
    tm = ROW_TILE
    tps = seq // tm
    hblocks = tm // CONV_HALO
    c_inner = w2.shape[0]
    return pl.pallas_call(
        functools.partial(_conformer_kernel, tiles_per_seq=tps),
        grid=(n // tm,),
        in_specs=[
            pl.BlockSpec((tm, D_MODEL), lambda i: (i, 0)),
            pl.BlockSpec((CONV_HALO, D_MODEL), lambda i: (jnp.maximum(i * hblocks - 1, 0), 0)),
            _resident(), _resident(), _resident(), _resident(), _resident(), _resident(),
            _resident(),
        ],
        out_specs=pl.BlockSpec((tm, D_MODEL), lambda i: (i, 0)),
        out_shape=jax.ShapeDtypeStruct((n, D_MODEL), _F32),
        scratch_shapes=[pltpu.VMEM((tm + CONV_HALO, D_MODEL), _BF16),
                        pltpu.VMEM((tm + CONV_HALO, c_inner), _F32),
                        pltpu.VMEM((tm, c_inner), _F32),
                        pltpu.VMEM((SUBLANES - 1, tm + CONV_HALO - SUBLANES, LANES), _F32)],
        compiler_params=_compiler_params(("arbitrary",)),
        name="conformer_conv",
    )(x2d, x2d, g, w1, b1, wdw, lng, w2, b2)


def _prep_fox(w_in, b_f, q_g, k_g):
    pad = GATE_COLS - N_HEADS
    w = jnp.pad(w_in, ((0, 0), (0, pad))).astype(_BF16)
    bf = jnp.pad(b_f, (0, pad)).reshape(1, GATE_COLS)
    head = jnp.arange(MXU_DIM) // HEAD_DIM
    bd = (head[:, None] == head[None, :]).astype(_BF16)
    return w, bf, q_g.reshape(1, D_MODEL), k_g.reshape(1, D_MODEL), bd


def _prep_ffn(w_up, w_dw, b_dw, w_down):
    def regroup(a):
        lead = a.shape[:-1]
        a = a.reshape(lead + (2, N_FFN_CHUNKS, FFN_CHUNK))
        a = jnp.moveaxis(a, -2, 0)
        return a.reshape((N_FFN_CHUNKS,) + lead + (2 * FFN_CHUNK,))

    wup = regroup(w_up).astype(_BF16)
    taps = regroup(w_dw)
    bias = regroup(b_dw[None, :])
    fill = jnp.zeros((N_FFN_CHUNKS, SUBLANES - FFN_TAPS - 1, 2 * FFN_CHUNK), _F32)
    wdw = jnp.concatenate([taps, bias, fill], axis=1)
    wdown = w_down.reshape(N_FFN_CHUNKS, FFN_CHUNK, D_MODEL).astype(_BF16)
    return wup, wdw, wdown


def kernel(x, fox_norm_g, fox_w_in, fox_b_f, fox_q_g, fox_k_g, fox_w_o, conv_norm_g, conv_w_pw1,
           conv_b_pw1, conv_w_dw, conv_b_dw, conv_ln_g, conv_w_pw2, conv_b_pw2, ffn_norm_g,
           ffn_w_up, ffn_w_dw, ffn_b_dw, ffn_w_down):
    batch, seq, d = x.shape
    depth = ffn_w_up.shape[0]
    x2d = x.reshape(batch * seq, d)
    for layer in range(depth):
        j = layer // 2
        if layer % 2 == 0:
            w, bf, qg, kg, bd = _prep_fox(fox_w_in[j], fox_b_f[j], fox_q_g[j], fox_k_g[j])
            qkv, cum, cumt = _fox_inproj(x2d, fox_norm_g[j].reshape(1, d), w, bf, qg, kg, bd,
                                         seq=seq)
            o = _fox_attn(qkv.reshape(batch, seq, 3 * d), cum.reshape(batch, seq, LANES), cumt)
            x2d = _outproj(x2d, o.reshape(batch * seq, d), fox_w_o[j].astype(_BF16))
        else:
            wdw = jnp.concatenate([conv_w_dw[j], conv_b_dw[j][None, :]], axis=0)
            x2d = _conformer(x2d, conv_norm_g[j].reshape(1, d), conv_w_pw1[j].astype(_BF16),
                             conv_b_pw1[j].reshape(1, -1), wdw, conv_ln_g[j].reshape(1, -1),
                             conv_w_pw2[j].astype(_BF16), conv_b_pw2[j].reshape(1, d), seq=seq)
        wup, wdw, wdown = _prep_ffn(ffn_w_up[layer], ffn_w_dw[layer], ffn_b_dw[layer],
                                    ffn_w_down[layer])
        x2d = _ffn(x2d, ffn_norm_g[layer].reshape(1, d), wup, wdw, wdown, seq=seq)
    return x2d.reshape(batch, seq, d)
```

```python
import functools

import jax
import jax.numpy as jnp
from jax import lax
from jax.experimental import pallas as pl
from jax.experimental.pallas import tpu as pltpu

D_MODEL = 1024
HEAD_DIM = 64
N_HEADS = D_MODEL // HEAD_DIM
HEADS_PER_BLOCK = 2
N_HEAD_BLOCKS = N_HEADS // HEADS_PER_BLOCK
FFN_DIM = 2816
FFN_TAPS = 3
CONV_TAPS = 31
EPS = 1e-6

LANES = 128
SUBLANES = 8
MXU_DIM = 256
VMEM_LIMIT_BYTES = 56 * 1024 * 1024

ROW_TILE = 1024
ATTN_TILE = 256
FFN_CHUNK = 256
N_FFN_CHUNKS = FFN_DIM // FFN_CHUNK
CONV_HALO = 32
FFN_HALO = 16
CONV_ROW_BLOCK = 64
GATE_COLS = LANES
MASK_VALUE = -1e30

_F32 = jnp.float32
_BF16 = jnp.bfloat16


def _rms_rows(x, gain):
    ms = jnp.mean(x * x, axis=-1, keepdims=True)
    return x * lax.rsqrt(ms + EPS) * gain


def _sigmoid(x):
    return 1.0 / (1.0 + jnp.exp(-x))


def _log_sigmoid(x):
    return jnp.minimum(x, 0.0) - jnp.log1p(jnp.exp(-jnp.abs(x)))


def _compiler_params(semantics):
    return pltpu.CompilerParams(dimension_semantics=semantics,
                                vmem_limit_bytes=VMEM_LIMIT_BYTES)


def _resident():
    return pl.BlockSpec(memory_space=pltpu.VMEM)


def _fox_inproj_kernel(x_ref, g_ref, w_ref, bf_ref, qg_ref, kg_ref, bd_ref,
                       qkv_ref, cum_ref, cumt_ref, h_ref, carry_ref, *, tiles_per_seq):
    i = pl.program_id(0)
    tm = x_ref.shape[0]
    h_ref[...] = _rms_rows(x_ref[...], g_ref[...]).astype(_BF16)

    n_chunk = 2 * MXU_DIM
    for c in range(3 * D_MODEL // n_chunk):
        cols = slice(c * n_chunk, (c + 1) * n_chunk)
        acc = jnp.dot(h_ref[...], w_ref[:, cols], preferred_element_type=_F32)
        if c * n_chunk < 2 * D_MODEL:
            is_q = c * n_chunk < D_MODEL
            gain_ref = qg_ref if is_q else kg_ref
            gcols = slice((c * n_chunk) % D_MODEL, (c * n_chunk) % D_MODEL + n_chunk)
            sq = (acc * acc).astype(_BF16)
            ssq = jnp.concatenate(
                [jnp.dot(sq[:, s * MXU_DIM:(s + 1) * MXU_DIM], bd_ref[...],
                         preferred_element_type=_F32) for s in range(n_chunk // MXU_DIM)],
                axis=1)
            acc = acc * lax.rsqrt(ssq * (1.0 / HEAD_DIM) + EPS) * gain_ref[:, gcols]
            if is_q:
                acc = acc * (HEAD_DIM ** -0.5)
        qkv_ref[:, cols] = acc.astype(_BF16)

    logit = jnp.dot(h_ref[...], w_ref[:, 3 * D_MODEL:3 * D_MODEL + GATE_COLS],
                    preferred_element_type=_F32) + bf_ref[...]
    c = _log_sigmoid(logit)
    row = lax.broadcasted_iota(jnp.int32, c.shape, 0)
    d = 1
    while d < tm:
        c = c + jnp.where(row >= d, pltpu.roll(c, d, axis=0), 0.0)
        d *= 2

    @pl.when(i % tiles_per_seq == 0)
    def _():
        carry_ref[...] = jnp.zeros_like(carry_ref)

    c = c + carry_ref[0:1, :]
    carry_ref[...] = jnp.broadcast_to(c[tm - 1:tm, :], carry_ref.shape)
    cum_ref[...] = c
    ct = c.T
    for hb in range(N_HEAD_BLOCKS):
        cumt_ref[0, hb] = ct[hb * HEADS_PER_BLOCK:(hb + 1) * HEADS_PER_BLOCK, :]


def _fox_inproj(x2d, g, w, bf, qg, kg, bd, *, seq):
    n = x2d.shape[0]
    tm = ROW_TILE
    tps = seq // tm
    batch = n // seq
    return pl.pallas_call(
        functools.partial(_fox_inproj_kernel, tiles_per_seq=tps),
        grid=(n // tm,),
        in_specs=[
            pl.BlockSpec((tm, D_MODEL), lambda i: (i, 0)),
            _resident(), _resident(), _resident(), _resident(), _resident(), _resident(),
        ],
        out_specs=[
            pl.BlockSpec((tm, 3 * D_MODEL), lambda i: (i, 0)),
            pl.BlockSpec((tm, LANES), lambda i: (i, 0)),
            pl.BlockSpec((1, N_HEAD_BLOCKS, HEADS_PER_BLOCK, tm),
                         lambda i: (i // tps, 0, 0, i % tps)),
        ],
        out_shape=[
            jax.ShapeDtypeStruct((n, 3 * D_MODEL), _BF16),
            jax.ShapeDtypeStruct((n, LANES), _F32),
            jax.ShapeDtypeStruct((batch, N_HEAD_BLOCKS, HEADS_PER_BLOCK, seq), _F32),
        ],
        scratch_shapes=[pltpu.VMEM((tm, D_MODEL), _BF16),
                        pltpu.VMEM((SUBLANES, LANES), _F32)],
        compiler_params=_compiler_params(("arbitrary",)),
        name="fox_inproj",
    )(x2d, g, w, bf, qg, kg, bd)


def _fox_attn_kernel(q_ref, k_ref, v_ref, cum_ref, cumt_ref, o_ref, qs_ref, s_ref, p_ref):
    hb = pl.program_id(1)
    seq = q_ref.shape[1]
    tq = ATTN_TILE
    tk = ATTN_TILE
    lane = lax.broadcasted_iota(jnp.int32, (tq, LANES), 1)
    causal = (lax.broadcasted_iota(jnp.int32, (tq, tk), 1)
              <= lax.broadcasted_iota(jnp.int32, (tq, tk), 0))
    heads = range(HEADS_PER_BLOCK)

    def fold(a):
        return [a[:, b * LANES:(b + 1) * LANES] for b in range(tk // LANES)]

    for i in reversed(range(seq // tq)):
        par = i % 2
        qrows = slice(i * tq, (i + 1) * tq)
        q = q_ref[0, qrows, :]
        cum = cum_ref[0, qrows, :]
        cq = []
        for e in heads:
            in_head = (lane >= e * HEAD_DIM) & (lane < (e + 1) * HEAD_DIM)
            qs_ref[par, e * tq:(e + 1) * tq, :] = jnp.where(in_head, q, jnp.zeros_like(q))
            head = hb * HEADS_PER_BLOCK + e
            cq.append(jnp.sum(jnp.where(lane == head, cum, 0.0), axis=1, keepdims=True))

        mx = [jnp.full((tq, LANES), MASK_VALUE, _F32) for _ in heads]
        for j in range(i + 1):
            kcols = slice(j * tk, (j + 1) * tk)
            s = lax.dot_general(qs_ref[par], k_ref[0, kcols, :], (((1,), (1,)), ((), ())),
                                preferred_element_type=_F32)
            for e in heads:
                se = s[e * tq:(e + 1) * tq, :] - cumt_ref[0, 0, e:e + 1, kcols]
                if j == i:
                    se = jnp.where(causal, se, MASK_VALUE)
                s_ref[par, e * tq:(e + 1) * tq, kcols] = se
                for part in fold(se):
                    mx[e] = jnp.maximum(mx[e], part)

        shift = []
        for e in heads:
            m = jnp.max(mx[e], axis=1, keepdims=True) + cq[e]
            shift.append(cq[e] - m)
        ls = [jnp.zeros((tq, LANES), _F32) for _ in heads]
        for j in range(i + 1):
            kcols = slice(j * tk, (j + 1) * tk)
            for e in heads:
                erows = slice(e * tq, (e + 1) * tq)
                p = jnp.exp(s_ref[par, erows, kcols] + shift[e])
                for part in fold(p):
                    ls[e] = ls[e] + part
                p_ref[par, erows, kcols] = p.astype(_BF16)

        q1 = (i + 1) * tq
        pv = jnp.dot(p_ref[par, :, :q1], v_ref[0, :q1, :], preferred_element_type=_F32)
        outs = [pv[e * tq:(e + 1) * tq, :] / jnp.sum(ls[e], axis=1, keepdims=True) for e in heads]
        o_ref[0, qrows, :] = jnp.where(lane < HEAD_DIM, outs[0], outs[1]).astype(o_ref.dtype)


def _fox_attn(qkv, cum, cumt):
    batch, seq, _ = qkv.shape
    rows = HEADS_PER_BLOCK * ATTN_TILE
    return pl.pallas_call(
        _fox_attn_kernel,
        grid=(batch, N_HEAD_BLOCKS),
        in_specs=[
            pl.BlockSpec((1, seq, LANES), lambda b, h: (b, 0, h)),
            pl.BlockSpec((1, seq, LANES), lambda b, h: (b, 0, N_HEAD_BLOCKS + h)),
            pl.BlockSpec((1, seq, LANES), lambda b, h: (b, 0, 2 * N_HEAD_BLOCKS + h)),
            pl.BlockSpec((1, seq, LANES), lambda b, h: (b, 0, 0)),
            pl.BlockSpec((1, 1, HEADS_PER_BLOCK, seq), lambda b, h: (b, h, 0, 0)),
        ],
        out_specs=pl.BlockSpec((1, seq, LANES), lambda b, h: (b, 0, h)),
        out_shape=jax.ShapeDtypeStruct((batch, seq, D_MODEL), _BF16),
        scratch_shapes=[pltpu.VMEM((2, rows, LANES), _BF16),
                        pltpu.VMEM((2, rows, seq), _F32),
                        pltpu.VMEM((2, rows, seq), _BF16)],
        compiler_params=_compiler_params(("arbitrary", "arbitrary")),
        name="fox_attn",
    )(qkv, qkv, qkv, cum, cumt)


def _ffn_kernel(*refs, tiles_per_seq, with_attn):
    if with_attn:
        (x_ref, xprev_ref, o_ref, oprev_ref, wo_ref, g_ref, wup_ref, wdw_ref, wdown_ref, y_ref,
         h_ref, u_ref, a_ref, oext_ref) = refs
    else:
        (x_ref, xprev_ref, g_ref, wup_ref, wdw_ref, wdown_ref, y_ref,
         h_ref, u_ref, a_ref) = refs
    i = pl.program_id(0)
    tm = x_ref.shape[0]
    halo = FFN_HALO
    if with_attn:
        oext_ref[:halo, :] = oprev_ref[...]
        oext_ref[halo:, :] = o_ref[...]
        upd = jnp.dot(oext_ref[...], wo_ref[...], preferred_element_type=_F32)
        x = x_ref[...] + upd[halo:, :]
        xprev = xprev_ref[...] + upd[:halo, :]
    else:
        x = x_ref[...]
        xprev = xprev_ref[...]
    y_ref[...] = x
    h_ref[halo:, :] = _rms_rows(x, g_ref[...]).astype(_BF16)
    hprev = _rms_rows(xprev, g_ref[...])
    hprev = jnp.where(i % tiles_per_seq == 0, 0.0, hprev)
    h_ref[:halo, :] = hprev.astype(_BF16)

    n_slab = FFN_CHUNK // LANES

    def cols(c, half, j=0, width=FFN_CHUNK):
        lo = half * FFN_DIM + c * FFN_CHUNK + j * LANES
        return slice(lo, lo + width)

    def up(c, slot):
        for half in range(2):
            u = jnp.dot(h_ref[...], wup_ref[:, cols(c, half)], preferred_element_type=_F32)
            for j in range(n_slab):
                u_ref[slot, half * n_slab + j] = u[:, j * LANES:(j + 1) * LANES]

    def conv(c, slot, half, j):
        ln = cols(c, half, j, LANES)
        y = wdw_ref[FFN_TAPS:FFN_TAPS + 1, ln]
        for k in range(FFN_TAPS):
            off = halo - (FFN_TAPS - 1) + k
            y = y + wdw_ref[k:k + 1, ln] * u_ref[slot, half * n_slab + j, off:off + tm, :]
        return y

    def act(c, slot):
        parts = []
        for j in range(n_slab):
            gate = conv(c, slot, 0, j)
            val = conv(c, slot, 1, j)
            parts.append((gate * _sigmoid(gate)) * val)
        a_ref[slot] = jnp.concatenate(parts, axis=1).astype(_BF16)

    def down(c, slot):
        rows = slice(c * FFN_CHUNK, (c + 1) * FFN_CHUNK)
        y_ref[...] += jnp.dot(a_ref[slot], wdown_ref[rows, :], preferred_element_type=_F32)

    up(0, 0)
    up(1, 1)
    act(0, 0)

    last = N_FFN_CHUNKS - 1
    for c in range(1, last):
        slot = c % 2
        up(c + 1, 1 - slot)
        down(c - 1, 1 - slot)
        act(c, slot)
    down(last - 1, (last - 1) % 2)
    act(last, last % 2)
    down(last, last % 2)


def _ffn(x2d, g, wup, wdw, wdown, *, seq, attn=None):
    n = x2d.shape[0]
    tm = ROW_TILE
    tps = seq // tm
    hblocks = tm // FFN_HALO
    row_spec = pl.BlockSpec((tm, D_MODEL), lambda i: (i, 0))
    halo_spec = pl.BlockSpec((FFN_HALO, D_MODEL), lambda i: (jnp.maximum(i * hblocks - 1, 0), 0))
    weights = [_resident()] * 4
    scratch = [pltpu.VMEM((tm + FFN_HALO, D_MODEL), _BF16),
               pltpu.VMEM((2, 2 * FFN_CHUNK // LANES, tm + FFN_HALO, LANES), _F32),
               pltpu.VMEM((2, tm, FFN_CHUNK), _BF16)]
    if attn is None:
        in_specs = [row_spec, halo_spec] + weights
        args = (x2d, x2d, g, wup, wdw, wdown)
    else:
        o2d, wo = attn
        in_specs = [row_spec, halo_spec, row_spec, halo_spec, _resident()] + weights
        args = (x2d, x2d, o2d, o2d, wo, g, wup, wdw, wdown)
        scratch = scratch + [pltpu.VMEM((tm + FFN_HALO, D_MODEL), _BF16)]
    return pl.pallas_call(
        functools.partial(_ffn_kernel, tiles_per_seq=tps, with_attn=attn is not None),
        grid=(n // tm,),
        in_specs=in_specs,
        out_specs=row_spec,
        out_shape=jax.ShapeDtypeStruct((n, D_MODEL), _F32),
        scratch_shapes=scratch,
        compiler_params=_compiler_params(("arbitrary",)),
        name="conv_ffn",
    )(*args)


def _conformer_kernel(x_ref, xprev_ref, g_ref, w1_ref, b1_ref, wdw_ref, lng_ref, w2_ref, b2_ref,
                      y_ref, h_ref, glu_ref, conv_ref, act_ref, *, tiles_per_seq):
    i = pl.program_id(0)
    tm = x_ref.shape[0]
    halo = CONV_HALO
    n_slab = glu_ref.shape[0]
    c_inner = n_slab * LANES
    slabs_per_block = MXU_DIM // LANES
    h_ref[halo:, :] = _rms_rows(x_ref[...], g_ref[...]).astype(_BF16)
    h_ref[:halo, :] = _rms_rows(xprev_ref[...], g_ref[...]).astype(_BF16)
    row = lax.broadcasted_iota(jnp.int32, (tm + halo, MXU_DIM), 0)
    keep = jnp.logical_or(row >= halo, i % tiles_per_seq != 0)

    def glu_block(cb):
        ca = slice(cb * MXU_DIM, (cb + 1) * MXU_DIM)
        cg = slice(c_inner + cb * MXU_DIM, c_inner + (cb + 1) * MXU_DIM)
        a = jnp.dot(h_ref[...], w1_ref[:, ca], preferred_element_type=_F32) + b1_ref[:, ca]
        g = jnp.dot(h_ref[...], w1_ref[:, cg], preferred_element_type=_F32) + b1_ref[:, cg]
        glu = jnp.where(keep, a * _sigmoid(g), 0.0)
        for j in range(slabs_per_block):
            glu_ref[cb * slabs_per_block + j] = glu[:, j * LANES:(j + 1) * LANES]

    rb = CONV_ROW_BLOCK
    first = halo - (CONV_TAPS - 1)

    def conv_slab(lt):
        ln = slice(lt * LANES, (lt + 1) * LANES)
        for base in range(0, tm, rb):
            acc = jnp.broadcast_to(wdw_ref[CONV_TAPS:CONV_TAPS + 1, ln], (rb, LANES))
            for k in range(CONV_TAPS):
                lo = base + first + k
                acc = acc + wdw_ref[k:k + 1, ln] * glu_ref[lt, lo:lo + rb, :]
            conv_ref[lt, base:base + rb, :] = acc

    n_blocks = n_slab // slabs_per_block
    glu_block(0)
    for cb in range(1, n_blocks + 1):
        if cb < n_blocks:
            glu_block(cb)
        for j in range(slabs_per_block):
            conv_slab((cb - 1) * slabs_per_block + j)

    ssq = conv_ref[0] * conv_ref[0]
    for lt in range(1, n_slab):
        ssq = ssq + conv_ref[lt] * conv_ref[lt]
    inv = lax.rsqrt(jnp.sum(ssq, axis=1, keepdims=True) * (1.0 / c_inner) + EPS)
    parts = []
    for lt in range(n_slab):
        ln = slice(lt * LANES, (lt + 1) * LANES)
        u = conv_ref[lt] * inv * lng_ref[:, ln]
        parts.append(u * _sigmoid(u))
    act_ref[...] = jnp.concatenate(parts, axis=1).astype(_BF16)
    y_ref[...] = (x_ref[...] + b2_ref[...]
                  + jnp.dot(act_ref[...], w2_ref[...], preferred_element_type=_F32))


def _conformer(x2d, g, w1, b1, wdw, lng, w2, b2, *, seq):
    n = x2d.shape[0]
    tm = ROW_TILE
    tps = seq // tm
    hblocks = tm // CONV_HALO
    c_inner = w2.shape[0]
    return pl.pallas_call(
        functools.partial(_conformer_kernel, tiles_per_seq=tps),
        grid=(n // tm,),
        in_specs=[
            pl.BlockSpec((tm, D_MODEL), lambda i: (i, 0)),
            pl.BlockSpec((CONV_HALO, D_MODEL), lambda i: (jnp.maximum(i * hblocks - 1, 0), 0)),
            _resident(), _resident(), _resident(), _resident(), _resident(), _resident(),
            _resident(),
        ],
        out_specs=pl.BlockSpec((tm, D_MODEL), lambda i: (i, 0)),
        out_shape=jax.ShapeDtypeStruct((n, D_MODEL), _F32),
        scratch_shapes=[pltpu.VMEM((tm + CONV_HALO, D_MODEL), _BF16),
                        pltpu.VMEM((c_inner // LANES, tm + CONV_HALO, LANES), _F32),
                        pltpu.VMEM((c_inner // LANES, tm, LANES), _F32),
                        pltpu.VMEM((tm, c_inner), _BF16)],
        compiler_params=_compiler_params(("arbitrary",)),
        name="conformer_conv",
    )(x2d, x2d, g, w1, b1, wdw, lng, w2, b2)


def _prep_fox(w_in, b_f, q_g, k_g):
    pad = GATE_COLS - N_HEADS
    w = jnp.pad(w_in, ((0, 0), (0, pad))).astype(_BF16)
    bf = jnp.pad(b_f, (0, pad)).reshape(1, GATE_COLS)
    head = jnp.arange(MXU_DIM) // HEAD_DIM
    bd = (head[:, None] == head[None, :]).astype(_BF16)
    return w, bf, q_g.reshape(1, D_MODEL), k_g.reshape(1, D_MODEL), bd


def _prep_ffn(w_up, w_dw, b_dw, w_down):
    fill = jnp.zeros((SUBLANES - FFN_TAPS - 1, w_dw.shape[1]), _F32)
    wdw = jnp.concatenate([w_dw, b_dw[None, :], fill], axis=0)
    return w_up.astype(_BF16), wdw, w_down.astype(_BF16)


def kernel(x, fox_norm_g, fox_w_in, fox_b_f, fox_q_g, fox_k_g, fox_w_o, conv_norm_g, conv_w_pw1,
           conv_b_pw1, conv_w_dw, conv_b_dw, conv_ln_g, conv_w_pw2, conv_b_pw2, ffn_norm_g,
           ffn_w_up, ffn_w_dw, ffn_b_dw, ffn_w_down):
    batch, seq, d = x.shape
    depth = ffn_w_up.shape[0]
    x2d = x.reshape(batch * seq, d)
    for layer in range(depth):
        j = layer // 2
        if layer % 2 == 0:
            w, bf, qg, kg, bd = _prep_fox(fox_w_in[j], fox_b_f[j], fox_q_g[j], fox_k_g[j])
            qkv, cum, cumt = _fox_inproj(x2d, fox_norm_g[j].reshape(1, d), w, bf, qg, kg, bd,
                                         seq=seq)
            o = _fox_attn(qkv.reshape(batch, seq, 3 * d), cum.reshape(batch, seq, LANES), cumt)
            attn = (o.reshape(batch * seq, d), fox_w_o[j].astype(_BF16))
        else:
            wdw = jnp.concatenate([conv_w_dw[j], conv_b_dw[j][None, :]], axis=0)
            attn = None
            x2d = _conformer(x2d, conv_norm_g[j].reshape(1, d), conv_w_pw1[j].astype(_BF16),
                             conv_b_pw1[j].reshape(1, -1), wdw, conv_ln_g[j].reshape(1, -1),
                             conv_w_pw2[j].astype(_BF16), conv_b_pw2[j].reshape(1, d), seq=seq)
        wup, wdw, wdown = _prep_ffn(ffn_w_up[layer], ffn_w_dw[layer], ffn_b_dw[layer],
                                    ffn_w_down[layer])
        x2d = _ffn(x2d, ffn_norm_g[layer].reshape(1, d), wup, wdw, wdown, seq=seq, attn=attn)
    return x2d.reshape(batch, seq, d)
```

```python
import functools

import jax
import jax.numpy as jnp
from jax import lax
from jax.experimental import pallas as pl
from jax.experimental.pallas import tpu as pltpu

D_MODEL = 1024
HEAD_DIM = 64
N_HEADS = D_MODEL // HEAD_DIM
HEADS_PER_BLOCK = 2
N_HEAD_BLOCKS = N_HEADS // HEADS_PER_BLOCK
FFN_DIM = 2816
FFN_TAPS = 3
CONV_TAPS = 31
EPS = 1e-6

LANES = 128
SUBLANES = 8
MXU_DIM = 256
VMEM_LIMIT_BYTES = 56 * 1024 * 1024

ROW_TILE = 1024
FUSED_TILE = 512
ATTN_TILE = 256
FFN_CHUNK = 256
N_FFN_CHUNKS = FFN_DIM // FFN_CHUNK
CONV_HALO = 32
FFN_HALO = 16
CONV_ROW_BLOCK = 64
GATE_COLS = LANES
MASK_VALUE = -1e30
LOG2_E = 1.4426950408889634

_F32 = jnp.float32
_BF16 = jnp.bfloat16


def _rms_rows(x, gain):
    ms = jnp.mean(x * x, axis=-1, keepdims=True)
    return x * lax.rsqrt(ms + EPS) * gain


def _sigmoid(x):
    return 1.0 / (1.0 + jnp.exp(-x))


def _log_sigmoid(x):
    return jnp.minimum(x, 0.0) - jnp.log1p(jnp.exp(-jnp.abs(x)))


def _compiler_params(semantics):
    return pltpu.CompilerParams(dimension_semantics=semantics,
                                vmem_limit_bytes=VMEM_LIMIT_BYTES)


def _resident():
    return pl.BlockSpec(memory_space=pltpu.VMEM)


def _fox_inproj_kernel(x_ref, g_ref, w_ref, bf_ref, qg_ref, kg_ref, bd_ref,
                       qkv_ref, cum_ref, cumt_ref, h_ref, carry_ref, *, tiles_per_seq):
    i = pl.program_id(0)
    tm = x_ref.shape[0]
    h_ref[...] = _rms_rows(x_ref[...], g_ref[...]).astype(_BF16)

    n_chunk = 2 * MXU_DIM
    for c in range(3 * D_MODEL // n_chunk):
        cols = slice(c * n_chunk, (c + 1) * n_chunk)
        acc = jnp.dot(h_ref[...], w_ref[:, cols], preferred_element_type=_F32)
        if c * n_chunk < 2 * D_MODEL:
            is_q = c * n_chunk < D_MODEL
            gain_ref = qg_ref if is_q else kg_ref
            gcols = slice((c * n_chunk) % D_MODEL, (c * n_chunk) % D_MODEL + n_chunk)
            sq = (acc * acc).astype(_BF16)
            ssq = jnp.concatenate(
                [jnp.dot(sq[:, s * MXU_DIM:(s + 1) * MXU_DIM], bd_ref[...],
                         preferred_element_type=_F32) for s in range(n_chunk // MXU_DIM)],
                axis=1)
            acc = acc * lax.rsqrt(ssq * (1.0 / HEAD_DIM) + EPS) * gain_ref[:, gcols]
            if is_q:
                acc = acc * (HEAD_DIM ** -0.5 * LOG2_E)
        qkv_ref[:, cols] = acc.astype(_BF16)

    logit = jnp.dot(h_ref[...], w_ref[:, 3 * D_MODEL:3 * D_MODEL + GATE_COLS],
                    preferred_element_type=_F32) + bf_ref[...]
    c = _log_sigmoid(logit)
    row = lax.broadcasted_iota(jnp.int32, c.shape, 0)
    d = 1
    while d < tm:
        c = c + jnp.where(row >= d, pltpu.roll(c, d, axis=0), 0.0)
        d *= 2

    @pl.when(i % tiles_per_seq == 0)
    def _():
        carry_ref[...] = jnp.zeros_like(carry_ref)

    c = c + carry_ref[0:1, :]
    carry_ref[...] = jnp.broadcast_to(c[tm - 1:tm, :], carry_ref.shape)
    c = c * LOG2_E
    cum_ref[...] = c
    ct = c.T
    for hb in range(N_HEAD_BLOCKS):
        cumt_ref[0, hb] = ct[hb * HEADS_PER_BLOCK:(hb + 1) * HEADS_PER_BLOCK, :]


def _fox_inproj(x2d, g, w, bf, qg, kg, bd, *, seq):
    n = x2d.shape[0]
    tm = ROW_TILE
    tps = seq // tm
    batch = n // seq
    return pl.pallas_call(
        functools.partial(_fox_inproj_kernel, tiles_per_seq=tps),
        grid=(n // tm,),
        in_specs=[
            pl.BlockSpec((tm, D_MODEL), lambda i: (i, 0)),
            _resident(), _resident(), _resident(), _resident(), _resident(), _resident(),
        ],
        out_specs=[
            pl.BlockSpec((tm, 3 * D_MODEL), lambda i: (i, 0)),
            pl.BlockSpec((tm, LANES), lambda i: (i, 0)),
            pl.BlockSpec((1, N_HEAD_BLOCKS, HEADS_PER_BLOCK, tm),
                         lambda i: (i // tps, 0, 0, i % tps)),
        ],
        out_shape=[
            jax.ShapeDtypeStruct((n, 3 * D_MODEL), _BF16),
            jax.ShapeDtypeStruct((n, LANES), _F32),
            jax.ShapeDtypeStruct((batch, N_HEAD_BLOCKS, HEADS_PER_BLOCK, seq), _F32),
        ],
        scratch_shapes=[pltpu.VMEM((tm, D_MODEL), _BF16),
                        pltpu.VMEM((SUBLANES, LANES), _F32)],
        compiler_params=_compiler_params(("arbitrary",)),
        name="fox_inproj",
    )(x2d, g, w, bf, qg, kg, bd)


def _fox_attn_kernel(q_ref, k_ref, v_ref, cum_ref, cumt_ref, o_ref, qs_ref, s_ref, p_ref):
    hb = pl.program_id(1)
    seq = q_ref.shape[1]
    tq = ATTN_TILE
    tk = ATTN_TILE
    lane = lax.broadcasted_iota(jnp.int32, (tq, LANES), 1)
    causal = (lax.broadcasted_iota(jnp.int32, (tq, tk), 1)
              <= lax.broadcasted_iota(jnp.int32, (tq, tk), 0))
    heads = range(HEADS_PER_BLOCK)

    def fold(a):
        return [a[:, b * LANES:(b + 1) * LANES] for b in range(tk // LANES)]

    for i in reversed(range(seq // tq)):
        par = i % 2
        qrows = slice(i * tq, (i + 1) * tq)
        q = q_ref[0, qrows, :]
        cum = cum_ref[0, qrows, :]
        cq = []
        for e in heads:
            in_head = (lane >= e * HEAD_DIM) & (lane < (e + 1) * HEAD_DIM)
            qs_ref[par, e * tq:(e + 1) * tq, :] = jnp.where(in_head, q, jnp.zeros_like(q))
            head = hb * HEADS_PER_BLOCK + e
            cq.append(jnp.sum(jnp.where(lane == head, cum, 0.0), axis=1, keepdims=True))

        mx = [jnp.full((tq, LANES), MASK_VALUE, _F32) for _ in heads]
        for j in range(i + 1):
            kcols = slice(j * tk, (j + 1) * tk)
            s = lax.dot_general(qs_ref[par], k_ref[0, kcols, :], (((1,), (1,)), ((), ())),
                                preferred_element_type=_F32)
            for e in heads:
                se = s[e * tq:(e + 1) * tq, :] - cumt_ref[0, 0, e:e + 1, kcols]
                if j == i:
                    se = jnp.where(causal, se, MASK_VALUE)
                s_ref[par, e * tq:(e + 1) * tq, kcols] = se
                for part in fold(se):
                    mx[e] = jnp.maximum(mx[e], part)

        shift = []
        for e in heads:
            m = jnp.max(mx[e], axis=1, keepdims=True) + cq[e]
            shift.append(cq[e] - m)
        ls = [jnp.zeros((tq, LANES), _F32) for _ in heads]
        for j in range(i + 1):
            kcols = slice(j * tk, (j + 1) * tk)
            for e in heads:
                erows = slice(e * tq, (e + 1) * tq)
                p = jnp.exp2(s_ref[par, erows, kcols] + shift[e])
                for part in fold(p):
                    ls[e] = ls[e] + part
                p_ref[par, erows, kcols] = p.astype(_BF16)

        q1 = (i + 1) * tq
        pv = jnp.dot(p_ref[par, :, :q1], v_ref[0, :q1, :], preferred_element_type=_F32)
        outs = [pv[e * tq:(e + 1) * tq, :] / jnp.sum(ls[e], axis=1, keepdims=True) for e in heads]
        o_ref[0, qrows, :] = jnp.where(lane < HEAD_DIM, outs[0], outs[1]).astype(o_ref.dtype)


def _fox_attn(qkv, cum, cumt):
    batch, seq, _ = qkv.shape
    rows = HEADS_PER_BLOCK * ATTN_TILE
    return pl.pallas_call(
        _fox_attn_kernel,
        grid=(batch, N_HEAD_BLOCKS),
        in_specs=[
            pl.BlockSpec((1, seq, LANES), lambda b, h: (b, 0, h)),
            pl.BlockSpec((1, seq, LANES), lambda b, h: (b, 0, N_HEAD_BLOCKS + h)),
            pl.BlockSpec((1, seq, LANES), lambda b, h: (b, 0, 2 * N_HEAD_BLOCKS + h)),
            pl.BlockSpec((1, seq, LANES), lambda b, h: (b, 0, 0)),
            pl.BlockSpec((1, 1, HEADS_PER_BLOCK, seq), lambda b, h: (b, h, 0, 0)),
        ],
        out_specs=pl.BlockSpec((1, seq, LANES), lambda b, h: (b, 0, h)),
        out_shape=jax.ShapeDtypeStruct((batch, seq, D_MODEL), _BF16),
        scratch_shapes=[pltpu.VMEM((2, rows, LANES), _BF16),
                        pltpu.VMEM((2, rows, seq), _F32),
                        pltpu.VMEM((2, rows, seq), _BF16)],
        compiler_params=_compiler_params(("arbitrary", "arbitrary")),
        name="fox_attn",
    )(qkv, qkv, qkv, cum, cumt)


def _ffn_steps(x, xprev, first, g_ref, wup_ref, wdw_ref, wdown_ref, y_ref, h_ref, u_ref, a_ref):
    tm = y_ref.shape[0]
    halo = FFN_HALO
    n_slab = FFN_CHUNK // LANES

    def prologue():
        xv = x()
        y_ref[...] = xv
        h_ref[halo:, :] = _rms_rows(xv, g_ref[...]).astype(_BF16)
        hprev = _rms_rows(xprev(), g_ref[...])
        h_ref[:halo, :] = jnp.where(first, 0.0, hprev).astype(_BF16)

    def cols(c, half, j=0, width=FFN_CHUNK):
        lo = half * FFN_DIM + c * FFN_CHUNK + j * LANES
        return slice(lo, lo + width)

    def up(c, slot):
        for half in range(2):
            u = jnp.dot(h_ref[...], wup_ref[:, cols(c, half)], preferred_element_type=_F32)
            for j in range(n_slab):
                u_ref[slot, half * n_slab + j] = u[:, j * LANES:(j + 1) * LANES]

    def conv(c, slot, half, j):
        ln = cols(c, half, j, LANES)
        y = wdw_ref[FFN_TAPS:FFN_TAPS + 1, ln]
        for k in range(FFN_TAPS):
            off = halo - (FFN_TAPS - 1) + k
            y = y + wdw_ref[k:k + 1, ln] * u_ref[slot, half * n_slab + j, off:off + tm, :]
        return y

    def act(c, slot):
        parts = []
        for j in range(n_slab):
            gate = conv(c, slot, 0, j)
            val = conv(c, slot, 1, j)
            parts.append((gate * _sigmoid(gate)) * val)
        a_ref[slot] = jnp.concatenate(parts, axis=1).astype(_BF16)

    def down(c, slot):
        rows = slice(c * FFN_CHUNK, (c + 1) * FFN_CHUNK)
        y_ref[...] += jnp.dot(a_ref[slot], wdown_ref[rows, :], preferred_element_type=_F32)

    def first_step():
        prologue()
        up(0, 0)

    def second_step():
        up(1, 1)
        act(0, 0)

    def middle_step(c):
        def run():
            slot = c % 2
            up(c + 1, 1 - slot)
            down(c - 1, 1 - slot)
            act(c, slot)
        return run

    last = N_FFN_CHUNKS - 1

    def last_step():
        down(last - 1, (last - 1) % 2)
        act(last, last % 2)
        down(last, last % 2)

    return [first_step, second_step] + [middle_step(c) for c in range(1, last)] + [last_step]


def _ffn_kernel(*refs, tiles_per_seq, with_attn):
    if with_attn:
        (x_ref, xprev_ref, o_ref, oprev_ref, wo_ref, g_ref, wup_ref, wdw_ref, wdown_ref, y_ref,
         h_ref, u_ref, a_ref, oext_ref) = refs
    else:
        (x_ref, xprev_ref, g_ref, wup_ref, wdw_ref, wdown_ref, y_ref,
         h_ref, u_ref, a_ref) = refs
    i = pl.program_id(0)
    halo = FFN_HALO
    if with_attn:
        oext_ref[:halo, :] = oprev_ref[...]
        oext_ref[halo:, :] = o_ref[...]
        upd = jnp.dot(oext_ref[...], wo_ref[...], preferred_element_type=_F32)
        x = lambda: x_ref[...] + upd[halo:, :]
        xprev = lambda: xprev_ref[...] + upd[:halo, :]
    else:
        x = lambda: x_ref[...]
        xprev = lambda: xprev_ref[...]
    for step in _ffn_steps(x, xprev, i % tiles_per_seq == 0, g_ref, wup_ref, wdw_ref, wdown_ref,
                           y_ref, h_ref, u_ref, a_ref):
        step()


def _ffn(x2d, g, wup, wdw, wdown, *, seq, attn=None):
    n = x2d.shape[0]
    tm = ROW_TILE
    tps = seq // tm
    hblocks = tm // FFN_HALO
    row_spec = pl.BlockSpec((tm, D_MODEL), lambda i: (i, 0))
    halo_spec = pl.BlockSpec((FFN_HALO, D_MODEL), lambda i: (jnp.maximum(i * hblocks - 1, 0), 0))
    weights = [_resident()] * 4
    scratch = [pltpu.VMEM((tm + FFN_HALO, D_MODEL), _BF16),
               pltpu.VMEM((2, 2 * FFN_CHUNK // LANES, tm + FFN_HALO, LANES), _F32),
               pltpu.VMEM((2, tm, FFN_CHUNK), _BF16)]
    if attn is None:
        in_specs = [row_spec, halo_spec] + weights
        args = (x2d, x2d, g, wup, wdw, wdown)
    else:
        o2d, wo = attn
        in_specs = [row_spec, halo_spec, row_spec, halo_spec, _resident()] + weights
        args = (x2d, x2d, o2d, o2d, wo, g, wup, wdw, wdown)
        scratch = scratch + [pltpu.VMEM((tm + FFN_HALO, D_MODEL), _BF16)]
    return pl.pallas_call(
        functools.partial(_ffn_kernel, tiles_per_seq=tps, with_attn=attn is not None),
        grid=(n // tm,),
        in_specs=in_specs,
        out_specs=row_spec,
        out_shape=jax.ShapeDtypeStruct((n, D_MODEL), _F32),
        scratch_shapes=scratch,
        compiler_params=_compiler_params(("arbitrary",)),
        name="conv_ffn",
    )(*args)


def _conformer_steps(x_ref, xprev_ref, first, g_ref, w1_ref, b1_ref, wdw_ref, lng_ref, w2_ref, b2_ref,
                     out_ref, h_ref, glu_ref, conv_ref, act_ref):
    tm = x_ref.shape[0]
    halo = CONV_HALO
    n_slab = glu_ref.shape[0]
    c_inner = n_slab * LANES
    slabs_per_block = MXU_DIM // LANES
    n_blocks = n_slab // slabs_per_block
    rb = CONV_ROW_BLOCK
    tap0 = halo - (CONV_TAPS - 1)

    def norm():
        h_ref[halo:, :] = _rms_rows(x_ref[...], g_ref[...]).astype(_BF16)
        h_ref[:halo, :] = _rms_rows(xprev_ref[...], g_ref[...]).astype(_BF16)

    def glu_block(cb):
        row = lax.broadcasted_iota(jnp.int32, (tm + halo, MXU_DIM), 0)
        keep = jnp.logical_or(row >= halo, jnp.logical_not(first))
        ca = slice(cb * MXU_DIM, (cb + 1) * MXU_DIM)
        cg = slice(c_inner + cb * MXU_DIM, c_inner + (cb + 1) * MXU_DIM)
        a = jnp.dot(h_ref[...], w1_ref[:, ca], preferred_element_type=_F32) + b1_ref[:, ca]
        g = jnp.dot(h_ref[...], w1_ref[:, cg], preferred_element_type=_F32) + b1_ref[:, cg]
        glu = jnp.where(keep, a * _sigmoid(g), 0.0)
        for j in range(slabs_per_block):
            glu_ref[cb * slabs_per_block + j] = glu[:, j * LANES:(j + 1) * LANES]

    def conv_slab(lt):
        ln = slice(lt * LANES, (lt + 1) * LANES)
        for base in range(0, tm, rb):
            acc = jnp.broadcast_to(wdw_ref[CONV_TAPS:CONV_TAPS + 1, ln], (rb, LANES))
            for k in range(CONV_TAPS):
                lo = base + tap0 + k
                acc = acc + wdw_ref[k:k + 1, ln] * glu_ref[lt, lo:lo + rb, :]
            conv_ref[lt, base:base + rb, :] = acc

    def tail():
        ssq = conv_ref[0] * conv_ref[0]
        for lt in range(1, n_slab):
            ssq = ssq + conv_ref[lt] * conv_ref[lt]
        inv = lax.rsqrt(jnp.sum(ssq, axis=1, keepdims=True) * (1.0 / c_inner) + EPS)
        parts = []
        for lt in range(n_slab):
            ln = slice(lt * LANES, (lt + 1) * LANES)
            u = conv_ref[lt] * inv * lng_ref[:, ln]
            parts.append(u * _sigmoid(u))
        act_ref[...] = jnp.concatenate(parts, axis=1).astype(_BF16)
        out_ref[...] = (x_ref[...] + b2_ref[...]
                        + jnp.dot(act_ref[...], w2_ref[...], preferred_element_type=_F32))

    def block_step(cb):
        def run():
            if cb == 0:
                norm()
            if cb < n_blocks:
                glu_block(cb)
            if cb >= 1:
                for j in range(slabs_per_block):
                    conv_slab((cb - 1) * slabs_per_block + j)
        return run

    return [block_step(cb) for cb in range(n_blocks + 1)] + [tail]


def _conformer_ffn_kernel(x_ref, xprev_ref, cg_ref, w1_ref, b1_ref, cdw_ref, lng_ref, w2_ref, b2_ref,
                          fg_ref, wup_ref, fdw_ref, wdown_ref, y_ref,
                          ch_ref, glu_ref, conv_ref, act_ref, mid_ref, midprev_ref,
                          fh_ref, u_ref, a_ref, *, tiles_per_seq):
    i = pl.program_id(0)

    @pl.when(i == 0)
    def _():
        mid_ref[...] = jnp.zeros_like(mid_ref)
        midprev_ref[...] = jnp.zeros_like(midprev_ref)

    mid_prev = midprev_ref[...]
    ffn_first = (i + tiles_per_seq - 1) % tiles_per_seq == 0
    ffn = _ffn_steps(lambda: mid_ref[...], lambda: mid_prev, ffn_first, fg_ref, wup_ref, fdw_ref,
                     wdown_ref, y_ref, fh_ref, u_ref, a_ref)
    conf = _conformer_steps(x_ref, xprev_ref, i % tiles_per_seq == 0, cg_ref, w1_ref, b1_ref, cdw_ref,
                            lng_ref, w2_ref, b2_ref, mid_ref, ch_ref, glu_ref, conv_ref, act_ref)
    ffn[0]()
    midprev_ref[...] = mid_ref[mid_ref.shape[0] - FFN_HALO:, :]
    per = -(-(len(ffn) - 1) // (len(conf) - 1))
    fi = 1
    for cstep in conf[:-1]:
        cstep()
        for fstep in ffn[fi:fi + per]:
            fstep()
        fi += per
    for fstep in ffn[fi:]:
        fstep()
    conf[-1]()


def _conformer_ffn(x2d, cg, w1, b1, cdw, lng, w2, b2, fg, wup, fdw, wdown, *, seq):
    n = x2d.shape[0]
    tm = FUSED_TILE
    tps = seq // tm
    n_tiles = n // tm
    hblocks = tm // CONV_HALO
    c_inner = w2.shape[0]
    last = n_tiles - 1
    return pl.pallas_call(
        functools.partial(_conformer_ffn_kernel, tiles_per_seq=tps),
        grid=(n_tiles + 1,),
        in_specs=[
            pl.BlockSpec((tm, D_MODEL), lambda i: (jnp.minimum(i, last), 0)),
            pl.BlockSpec((CONV_HALO, D_MODEL),
                         lambda i: (jnp.maximum(jnp.minimum(i, last) * hblocks - 1, 0), 0)),
        ] + [_resident()] * 11,
        out_specs=pl.BlockSpec((tm, D_MODEL), lambda i: (jnp.maximum(i - 1, 0), 0)),
        out_shape=jax.ShapeDtypeStruct((n, D_MODEL), _F32),
        scratch_shapes=[pltpu.VMEM((tm + CONV_HALO, D_MODEL), _BF16),
                        pltpu.VMEM((c_inner // LANES, tm + CONV_HALO, LANES), _F32),
                        pltpu.VMEM((c_inner // LANES, tm, LANES), _F32),
                        pltpu.VMEM((tm, c_inner), _BF16),
                        pltpu.VMEM((tm, D_MODEL), _F32),
                        pltpu.VMEM((FFN_HALO, D_MODEL), _F32),
                        pltpu.VMEM((tm + FFN_HALO, D_MODEL), _BF16),
                        pltpu.VMEM((2, 2 * FFN_CHUNK // LANES, tm + FFN_HALO, LANES), _F32),
                        pltpu.VMEM((2, tm, FFN_CHUNK), _BF16)],
        compiler_params=_compiler_params(("arbitrary",)),
        name="conformer_ffn",
    )(x2d, x2d, cg, w1, b1, cdw, lng, w2, b2, fg, wup, fdw, wdown)


def _prep_fox(w_in, b_f, q_g, k_g):
    pad = GATE_COLS - N_HEADS
    w = jnp.pad(w_in, ((0, 0), (0, pad))).astype(_BF16)
    bf = jnp.pad(b_f, (0, pad)).reshape(1, GATE_COLS)
    head = jnp.arange(MXU_DIM) // HEAD_DIM
    bd = (head[:, None] == head[None, :]).astype(_BF16)
    return w, bf, q_g.reshape(1, D_MODEL), k_g.reshape(1, D_MODEL), bd


def _prep_ffn(w_up, w_dw, b_dw, w_down):
    fill = jnp.zeros((SUBLANES - FFN_TAPS - 1, w_dw.shape[1]), _F32)
    wdw = jnp.concatenate([w_dw, b_dw[None, :], fill], axis=0)
    return w_up.astype(_BF16), wdw, w_down.astype(_BF16)


def kernel(x, fox_norm_g, fox_w_in, fox_b_f, fox_q_g, fox_k_g, fox_w_o, conv_norm_g, conv_w_pw1,
           conv_b_pw1, conv_w_dw, conv_b_dw, conv_ln_g, conv_w_pw2, conv_b_pw2, ffn_norm_g,
           ffn_w_up, ffn_w_dw, ffn_b_dw, ffn_w_down):
    batch, seq, d = x.shape
    depth = ffn_w_up.shape[0]
    x2d = x.reshape(batch * seq, d)
    for layer in range(depth):
        j = layer // 2
        wup, fdw, wdown = _prep_ffn(ffn_w_up[layer], ffn_w_dw[layer], ffn_b_dw[layer],
                                    ffn_w_down[layer])
        fg = ffn_norm_g[layer].reshape(1, d)
        if layer % 2 == 0:
            w, bf, qg, kg, bd = _prep_fox(fox_w_in[j], fox_b_f[j], fox_q_g[j], fox_k_g[j])
            qkv, cum, cumt = _fox_inproj(x2d, fox_norm_g[j].reshape(1, d), w, bf, qg, kg, bd,
                                         seq=seq)
            o = _fox_attn(qkv.reshape(batch, seq, 3 * d), cum.reshape(batch, seq, LANES), cumt)
            attn = (o.reshape(batch * seq, d), fox_w_o[j].astype(_BF16))
            x2d = _ffn(x2d, fg, wup, fdw, wdown, seq=seq, attn=attn)
        else:
            cdw = jnp.concatenate([conv_w_dw[j], conv_b_dw[j][None, :]], axis=0)
            x2d = _conformer_ffn(x2d, conv_norm_g[j].reshape(1, d), conv_w_pw1[j].astype(_BF16),
                                 conv_b_pw1[j].reshape(1, -1), cdw, conv_ln_g[j].reshape(1, -1),
                                 conv_w_pw2[j].astype(_BF16), conv_b_pw2[j].reshape(1, d),
                                 fg, wup, fdw, wdown, seq=seq)
    return x2d.reshape(batch, seq, d)
```

```python
import functools

import jax
import jax.numpy as jnp
from jax import lax
from jax.experimental import pallas as pl
from jax.experimental.pallas import tpu as pltpu

D_MODEL = 1024
HEAD_DIM = 64
N_HEADS = D_MODEL // HEAD_DIM
HEADS_PER_BLOCK = 2
N_HEAD_BLOCKS = N_HEADS // HEADS_PER_BLOCK
FFN_DIM = 2816
FFN_TAPS = 3
CONV_TAPS = 31
EPS = 1e-6

LANES = 128
SUBLANES = 8
MXU_DIM = 256
VMEM_LIMIT_BYTES = 56 * 1024 * 1024

ROW_TILE = 1024
FUSED_TILE = 512
ATTN_TILE = 256
FFN_CHUNK = 256
N_FFN_CHUNKS = FFN_DIM // FFN_CHUNK
CONV_HALO = 32
FFN_HALO = 16
CONV_ROW_BLOCK = 32
FFN_STEPS_PER_CONV_BLOCK = 4
GATE_COLS = LANES
MASK_VALUE = -1e30
LOG2_E = 1.4426950408889634

_F32 = jnp.float32
_BF16 = jnp.bfloat16


def _rms_rows(x, gain):
    ms = jnp.mean(x * x, axis=-1, keepdims=True)
    return x * lax.rsqrt(ms + EPS) * gain


def _sigmoid(x):
    return 1.0 / (1.0 + jnp.exp(-x))


def _log_sigmoid(x):
    return jnp.minimum(x, 0.0) - jnp.log1p(jnp.exp(-jnp.abs(x)))


def _compiler_params(semantics):
    return pltpu.CompilerParams(dimension_semantics=semantics,
                                vmem_limit_bytes=VMEM_LIMIT_BYTES)


def _resident():
    return pl.BlockSpec(memory_space=pltpu.VMEM)


def _fox_inproj_kernel(x_ref, g_ref, w_ref, bf_ref, qg_ref, kg_ref, bd_ref,
                       qkv_ref, cum_ref, cumt_ref, h_ref, carry_ref, *, tiles_per_seq):
    i = pl.program_id(0)
    tm = x_ref.shape[0]
    h_ref[...] = _rms_rows(x_ref[...], g_ref[...]).astype(_BF16)

    n_chunk = 2 * MXU_DIM
    for c in range(3 * D_MODEL // n_chunk):
        cols = slice(c * n_chunk, (c + 1) * n_chunk)
        acc = jnp.dot(h_ref[...], w_ref[:, cols], preferred_element_type=_F32)
        if c * n_chunk < 2 * D_MODEL:
            is_q = c * n_chunk < D_MODEL
            gain_ref = qg_ref if is_q else kg_ref
            gcols = slice((c * n_chunk) % D_MODEL, (c * n_chunk) % D_MODEL + n_chunk)
            sq = (acc * acc).astype(_BF16)
            ssq = jnp.concatenate(
                [jnp.dot(sq[:, s * MXU_DIM:(s + 1) * MXU_DIM], bd_ref[...],
                         preferred_element_type=_F32) for s in range(n_chunk // MXU_DIM)],
                axis=1)
            acc = acc * lax.rsqrt(ssq * (1.0 / HEAD_DIM) + EPS) * gain_ref[:, gcols]
            if is_q:
                acc = acc * (HEAD_DIM ** -0.5 * LOG2_E)
        qkv_ref[:, cols] = acc.astype(_BF16)

    logit = jnp.dot(h_ref[...], w_ref[:, 3 * D_MODEL:3 * D_MODEL + GATE_COLS],
                    preferred_element_type=_F32) + bf_ref[...]
    c = _log_sigmoid(logit)
    row = lax.broadcasted_iota(jnp.int32, c.shape, 0)
    d = 1
    while d < tm:
        c = c + jnp.where(row >= d, pltpu.roll(c, d, axis=0), 0.0)
        d *= 2

    @pl.when(i % tiles_per_seq == 0)
    def _():
        carry_ref[...] = jnp.zeros_like(carry_ref)

    c = c + carry_ref[0:1, :]
    carry_ref[...] = jnp.broadcast_to(c[tm - 1:tm, :], carry_ref.shape)
    c = c * LOG2_E
    cum_ref[...] = c
    ct = c.T
    for hb in range(N_HEAD_BLOCKS):
        cumt_ref[0, hb] = ct[hb * HEADS_PER_BLOCK:(hb + 1) * HEADS_PER_BLOCK, :]


def _fox_inproj(x2d, g, w, bf, qg, kg, bd, *, seq):
    n = x2d.shape[0]
    tm = ROW_TILE
    tps = seq // tm
    batch = n // seq
    return pl.pallas_call(
        functools.partial(_fox_inproj_kernel, tiles_per_seq=tps),
        grid=(n // tm,),
        in_specs=[
            pl.BlockSpec((tm, D_MODEL), lambda i: (i, 0)),
            _resident(), _resident(), _resident(), _resident(), _resident(), _resident(),
        ],
        out_specs=[
            pl.BlockSpec((tm, 3 * D_MODEL), lambda i: (i, 0)),
            pl.BlockSpec((tm, LANES), lambda i: (i, 0)),
            pl.BlockSpec((1, N_HEAD_BLOCKS, HEADS_PER_BLOCK, tm),
                         lambda i: (i // tps, 0, 0, i % tps)),
        ],
        out_shape=[
            jax.ShapeDtypeStruct((n, 3 * D_MODEL), _BF16),
            jax.ShapeDtypeStruct((n, LANES), _F32),
            jax.ShapeDtypeStruct((batch, N_HEAD_BLOCKS, HEADS_PER_BLOCK, seq), _F32),
        ],
        scratch_shapes=[pltpu.VMEM((tm, D_MODEL), _BF16),
                        pltpu.VMEM((SUBLANES, LANES), _F32)],
        compiler_params=_compiler_params(("arbitrary",)),
        name="fox_inproj",
    )(x2d, g, w, bf, qg, kg, bd)


def _fox_attn_kernel(q_ref, k_ref, v_ref, cum_ref, cumt_ref, o_ref, qs_ref, s_ref, p_ref):
    hb = pl.program_id(1)
    seq = q_ref.shape[1]
    tq = ATTN_TILE
    tk = ATTN_TILE
    lane = lax.broadcasted_iota(jnp.int32, (tq, LANES), 1)
    causal = (lax.broadcasted_iota(jnp.int32, (tq, tk), 1)
              <= lax.broadcasted_iota(jnp.int32, (tq, tk), 0))
    heads = range(HEADS_PER_BLOCK)

    def fold(a):
        return [a[:, b * LANES:(b + 1) * LANES] for b in range(tk // LANES)]

    for i in reversed(range(seq // tq)):
        par = i % 2
        qrows = slice(i * tq, (i + 1) * tq)
        q = q_ref[0, qrows, :]
        cum = cum_ref[0, qrows, :]
        cq = []
        for e in heads:
            in_head = (lane >= e * HEAD_DIM) & (lane < (e + 1) * HEAD_DIM)
            qs_ref[par, e * tq:(e + 1) * tq, :] = jnp.where(in_head, q, jnp.zeros_like(q))
            head = hb * HEADS_PER_BLOCK + e
            cq.append(jnp.sum(jnp.where(lane == head, cum, 0.0), axis=1, keepdims=True))

        mx = [jnp.full((tq, LANES), MASK_VALUE, _F32) for _ in heads]
        for j in range(i + 1):
            kcols = slice(j * tk, (j + 1) * tk)
            s = lax.dot_general(qs_ref[par], k_ref[0, kcols, :], (((1,), (1,)), ((), ())),
                                preferred_element_type=_F32)
            for e in heads:
                se = s[e * tq:(e + 1) * tq, :] - cumt_ref[0, 0, e:e + 1, kcols]
                if j == i:
                    se = jnp.where(causal, se, MASK_VALUE)
                s_ref[par, e * tq:(e + 1) * tq, kcols] = se
                for part in fold(se):
                    mx[e] = jnp.maximum(mx[e], part)

        shift = []
        for e in heads:
            m = jnp.max(mx[e], axis=1, keepdims=True) + cq[e]
            shift.append(cq[e] - m)
        ls = [jnp.zeros((tq, LANES), _F32) for _ in heads]
        for j in range(i + 1):
            kcols = slice(j * tk, (j + 1) * tk)
            for e in heads:
                erows = slice(e * tq, (e + 1) * tq)
                p = jnp.exp2(s_ref[par, erows, kcols] + shift[e])
                for part in fold(p):
                    ls[e] = ls[e] + part
                p_ref[par, erows, kcols] = p.astype(_BF16)

        q1 = (i + 1) * tq
        pv = jnp.dot(p_ref[par, :, :q1], v_ref[0, :q1, :], preferred_element_type=_F32)
        outs = [pv[e * tq:(e + 1) * tq, :] / jnp.sum(ls[e], axis=1, keepdims=True) for e in heads]
        o_ref[0, qrows, :] = jnp.where(lane < HEAD_DIM, outs[0], outs[1]).astype(o_ref.dtype)


def _fox_attn(qkv, cum, cumt):
    batch, seq, _ = qkv.shape
    rows = HEADS_PER_BLOCK * ATTN_TILE
    return pl.pallas_call(
        _fox_attn_kernel,
        grid=(batch, N_HEAD_BLOCKS),
        in_specs=[
            pl.BlockSpec((1, seq, LANES), lambda b, h: (b, 0, h)),
            pl.BlockSpec((1, seq, LANES), lambda b, h: (b, 0, N_HEAD_BLOCKS + h)),
            pl.BlockSpec((1, seq, LANES), lambda b, h: (b, 0, 2 * N_HEAD_BLOCKS + h)),
            pl.BlockSpec((1, seq, LANES), lambda b, h: (b, 0, 0)),
            pl.BlockSpec((1, 1, HEADS_PER_BLOCK, seq), lambda b, h: (b, h, 0, 0)),
        ],
        out_specs=pl.BlockSpec((1, seq, LANES), lambda b, h: (b, 0, h)),
        out_shape=jax.ShapeDtypeStruct((batch, seq, D_MODEL), _BF16),
        scratch_shapes=[pltpu.VMEM((2, rows, LANES), _BF16),
                        pltpu.VMEM((2, rows, seq), _F32),
                        pltpu.VMEM((2, rows, seq), _BF16)],
        compiler_params=_compiler_params(("arbitrary", "arbitrary")),
        name="fox_attn",
    )(qkv, qkv, qkv, cum, cumt)


def _ffn_steps(x, xprev, first, g_ref, wup_ref, wdw_ref, wdown_ref, y_ref, h_ref, u_ref, a_ref):
    tm = y_ref.shape[0]
    halo = FFN_HALO
    n_slab = FFN_CHUNK // LANES

    def prologue():
        xv = x()
        y_ref[...] = xv
        h_ref[halo:, :] = _rms_rows(xv, g_ref[...]).astype(_BF16)
        hprev = _rms_rows(xprev(), g_ref[...])
        h_ref[:halo, :] = jnp.where(first, 0.0, hprev).astype(_BF16)

    def cols(c, half, j=0, width=FFN_CHUNK):
        lo = half * FFN_DIM + c * FFN_CHUNK + j * LANES
        return slice(lo, lo + width)

    def up(c, slot):
        for half in range(2):
            u = jnp.dot(h_ref[...], wup_ref[:, cols(c, half)], preferred_element_type=_F32)
            for j in range(n_slab):
                u_ref[slot, half * n_slab + j] = u[:, j * LANES:(j + 1) * LANES]

    def conv(c, slot, half, j):
        ln = cols(c, half, j, LANES)
        y = wdw_ref[FFN_TAPS:FFN_TAPS + 1, ln]
        for k in range(FFN_TAPS):
            off = halo - (FFN_TAPS - 1) + k
            y = y + wdw_ref[k:k + 1, ln] * u_ref[slot, half * n_slab + j, off:off + tm, :]
        return y

    def act(c, slot):
        parts = []
        for j in range(n_slab):
            gate = conv(c, slot, 0, j)
            val = conv(c, slot, 1, j)
            parts.append((gate * _sigmoid(gate)) * val)
        a_ref[slot] = jnp.concatenate(parts, axis=1).astype(_BF16)

    def down(c, slot):
        rows = slice(c * FFN_CHUNK, (c + 1) * FFN_CHUNK)
        y_ref[...] += jnp.dot(a_ref[slot], wdown_ref[rows, :], preferred_element_type=_F32)

    def first_step():
        prologue()
        up(0, 0)

    def second_step():
        up(1, 1)
        act(0, 0)

    def middle_step(c):
        def run():
            slot = c % 2
            up(c + 1, 1 - slot)
            down(c - 1, 1 - slot)
            act(c, slot)
        return run

    last = N_FFN_CHUNKS - 1

    def last_step():
        down(last - 1, (last - 1) % 2)
        act(last, last % 2)
        down(last, last % 2)

    return [first_step, second_step] + [middle_step(c) for c in range(1, last)] + [last_step]


def _ffn_kernel(*refs, tiles_per_seq, with_attn):
    if with_attn:
        (x_ref, xprev_ref, o_ref, oprev_ref, wo_ref, g_ref, wup_ref, wdw_ref, wdown_ref, y_ref,
         h_ref, u_ref, a_ref, oext_ref) = refs
    else:
        (x_ref, xprev_ref, g_ref, wup_ref, wdw_ref, wdown_ref, y_ref,
         h_ref, u_ref, a_ref) = refs
    i = pl.program_id(0)
    halo = FFN_HALO
    if with_attn:
        oext_ref[:halo, :] = oprev_ref[...]
        oext_ref[halo:, :] = o_ref[...]
        upd = jnp.dot(oext_ref[...], wo_ref[...], preferred_element_type=_F32)
        x = lambda: x_ref[...] + upd[halo:, :]
        xprev = lambda: xprev_ref[...] + upd[:halo, :]
    else:
        x = lambda: x_ref[...]
        xprev = lambda: xprev_ref[...]
    for step in _ffn_steps(x, xprev, i % tiles_per_seq == 0, g_ref, wup_ref, wdw_ref, wdown_ref,
                           y_ref, h_ref, u_ref, a_ref):
        step()


def _ffn(x2d, g, wup, wdw, wdown, *, seq, attn=None):
    n = x2d.shape[0]
    tm = ROW_TILE
    tps = seq // tm
    hblocks = tm // FFN_HALO
    row_spec = pl.BlockSpec((tm, D_MODEL), lambda i: (i, 0))
    halo_spec = pl.BlockSpec((FFN_HALO, D_MODEL), lambda i: (jnp.maximum(i * hblocks - 1, 0), 0))
    weights = [_resident()] * 4
    scratch = [pltpu.VMEM((tm + FFN_HALO, D_MODEL), _BF16),
               pltpu.VMEM((2, 2 * FFN_CHUNK // LANES, tm + FFN_HALO, LANES), _F32),
               pltpu.VMEM((2, tm, FFN_CHUNK), _BF16)]
    if attn is None:
        in_specs = [row_spec, halo_spec] + weights
        args = (x2d, x2d, g, wup, wdw, wdown)
    else:
        o2d, wo = attn
        in_specs = [row_spec, halo_spec, row_spec, halo_spec, _resident()] + weights
        args = (x2d, x2d, o2d, o2d, wo, g, wup, wdw, wdown)
        scratch = scratch + [pltpu.VMEM((tm + FFN_HALO, D_MODEL), _BF16)]
    return pl.pallas_call(
        functools.partial(_ffn_kernel, tiles_per_seq=tps, with_attn=attn is not None),
        grid=(n // tm,),
        in_specs=in_specs,
        out_specs=row_spec,
        out_shape=jax.ShapeDtypeStruct((n, D_MODEL), _F32),
        scratch_shapes=scratch,
        compiler_params=_compiler_params(("arbitrary",)),
        name="conv_ffn",
    )(*args)


def _conformer_steps(x_ref, xprev_ref, first, g_ref, w1_ref, b1_ref, wdw_ref, lng_ref, w2_ref, b2_ref,
                     out_ref, h_ref, glu_ref, conv_ref, act_ref):
    tm = x_ref.shape[0]
    halo = CONV_HALO
    n_slab = glu_ref.shape[0]
    c_inner = n_slab * LANES
    slabs_per_block = MXU_DIM // LANES
    n_blocks = n_slab // slabs_per_block
    rb = CONV_ROW_BLOCK
    tap0 = halo - (CONV_TAPS - 1)

    def norm():
        h_ref[halo:, :] = _rms_rows(x_ref[...], g_ref[...]).astype(_BF16)
        h_ref[:halo, :] = _rms_rows(xprev_ref[...], g_ref[...]).astype(_BF16)

    def glu_block(cb):
        row = lax.broadcasted_iota(jnp.int32, (tm + halo, MXU_DIM), 0)
        keep = jnp.logical_or(row >= halo, jnp.logical_not(first))
        ca = slice(cb * MXU_DIM, (cb + 1) * MXU_DIM)
        cg = slice(c_inner + cb * MXU_DIM, c_inner + (cb + 1) * MXU_DIM)
        a = jnp.dot(h_ref[...], w1_ref[:, ca], preferred_element_type=_F32) + b1_ref[:, ca]
        g = jnp.dot(h_ref[...], w1_ref[:, cg], preferred_element_type=_F32) + b1_ref[:, cg]
        glu = jnp.where(keep, a * _sigmoid(g), 0.0)
        for j in range(slabs_per_block):
            glu_ref[cb * slabs_per_block + j] = glu[:, j * LANES:(j + 1) * LANES]

    def conv_slab(lt):
        ln = slice(lt * LANES, (lt + 1) * LANES)
        for base in range(0, tm, rb):
            acc = jnp.broadcast_to(wdw_ref[CONV_TAPS:CONV_TAPS + 1, ln], (rb, LANES))
            for k in range(CONV_TAPS):
                lo = base + tap0 + k
                acc = acc + wdw_ref[k:k + 1, ln] * glu_ref[lt, lo:lo + rb, :]
            conv_ref[lt, base:base + rb, :] = acc

    def tail():
        ssq = conv_ref[0] * conv_ref[0]
        for lt in range(1, n_slab):
            ssq = ssq + conv_ref[lt] * conv_ref[lt]
        inv = lax.rsqrt(jnp.sum(ssq, axis=1, keepdims=True) * (1.0 / c_inner) + EPS)
        parts = []
        for lt in range(n_slab):
            ln = slice(lt * LANES, (lt + 1) * LANES)
            u = conv_ref[lt] * inv * lng_ref[:, ln]
            parts.append(u * _sigmoid(u))
        act_ref[...] = jnp.concatenate(parts, axis=1).astype(_BF16)
        out_ref[...] = (x_ref[...] + b2_ref[...]
                        + jnp.dot(act_ref[...], w2_ref[...], preferred_element_type=_F32))

    def block_step(cb):
        def run():
            if cb == 0:
                norm()
            if cb < n_blocks:
                glu_block(cb)
            if cb >= 1:
                for j in range(slabs_per_block):
                    conv_slab((cb - 1) * slabs_per_block + j)
        return run

    return [block_step(cb) for cb in range(n_blocks + 1)] + [tail]


def _conformer_ffn_kernel(x_ref, xprev_ref, cg_ref, w1_ref, b1_ref, cdw_ref, lng_ref, w2_ref, b2_ref,
                          fg_ref, wup_ref, fdw_ref, wdown_ref, y_ref,
                          ch_ref, glu_ref, conv_ref, act_ref, mid_ref, midprev_ref,
                          fh_ref, u_ref, a_ref, *, tiles_per_seq):
    i = pl.program_id(0)

    @pl.when(i == 0)
    def _():
        mid_ref[...] = jnp.zeros_like(mid_ref)
        midprev_ref[...] = jnp.zeros_like(midprev_ref)

    mid_prev = midprev_ref[...]
    ffn_first = (i + tiles_per_seq - 1) % tiles_per_seq == 0
    ffn = _ffn_steps(lambda: mid_ref[...], lambda: mid_prev, ffn_first, fg_ref, wup_ref, fdw_ref,
                     wdown_ref, y_ref, fh_ref, u_ref, a_ref)
    conf = _conformer_steps(x_ref, xprev_ref, i % tiles_per_seq == 0, cg_ref, w1_ref, b1_ref, cdw_ref,
                            lng_ref, w2_ref, b2_ref, mid_ref, ch_ref, glu_ref, conv_ref, act_ref)
    ffn[0]()
    midprev_ref[...] = mid_ref[mid_ref.shape[0] - FFN_HALO:, :]
    per = FFN_STEPS_PER_CONV_BLOCK
    fi = 1
    for cstep in conf[:-1]:
        cstep()
        for fstep in ffn[fi:fi + per]:
            fstep()
        fi += per
    for fstep in ffn[fi:]:
        fstep()
    conf[-1]()


def _conformer_ffn(x2d, cg, w1, b1, cdw, lng, w2, b2, fg, wup, fdw, wdown, *, seq):
    n = x2d.shape[0]
    tm = FUSED_TILE
    tps = seq // tm
    n_tiles = n // tm
    hblocks = tm // CONV_HALO
    c_inner = w2.shape[0]
    last = n_tiles - 1
    return pl.pallas_call(
        functools.partial(_conformer_ffn_kernel, tiles_per_seq=tps),
        grid=(n_tiles + 1,),
        in_specs=[
            pl.BlockSpec((tm, D_MODEL), lambda i: (jnp.minimum(i, last), 0)),
            pl.BlockSpec((CONV_HALO, D_MODEL),
                         lambda i: (jnp.maximum(jnp.minimum(i, last) * hblocks - 1, 0), 0)),
        ] + [_resident()] * 11,
        out_specs=pl.BlockSpec((tm, D_MODEL), lambda i: (jnp.maximum(i - 1, 0), 0)),
        out_shape=jax.ShapeDtypeStruct((n, D_MODEL), _F32),
        scratch_shapes=[pltpu.VMEM((tm + CONV_HALO, D_MODEL), _BF16),
                        pltpu.VMEM((c_inner // LANES, tm + CONV_HALO, LANES), _F32),
                        pltpu.VMEM((c_inner // LANES, tm, LANES), _F32),
                        pltpu.VMEM((tm, c_inner), _BF16),
                        pltpu.VMEM((tm, D_MODEL), _F32),
                        pltpu.VMEM((FFN_HALO, D_MODEL), _F32),
                        pltpu.VMEM((tm + FFN_HALO, D_MODEL), _BF16),
                        pltpu.VMEM((2, 2 * FFN_CHUNK // LANES, tm + FFN_HALO, LANES), _F32),
                        pltpu.VMEM((2, tm, FFN_CHUNK), _BF16)],
        compiler_params=_compiler_params(("arbitrary",)),
        name="conformer_ffn",
    )(x2d, x2d, cg, w1, b1, cdw, lng, w2, b2, fg, wup, fdw, wdown)


def _prep_fox(w_in, b_f, q_g, k_g):
    pad = GATE_COLS - N_HEADS
    w = jnp.pad(w_in, ((0, 0), (0, pad))).astype(_BF16)
    bf = jnp.pad(b_f, (0, pad)).reshape(1, GATE_COLS)
    head = jnp.arange(MXU_DIM) // HEAD_DIM
    bd = (head[:, None] == head[None, :]).astype(_BF16)
    return w, bf, q_g.reshape(1, D_MODEL), k_g.reshape(1, D_MODEL), bd


def _prep_ffn(w_up, w_dw, b_dw, w_down):
    fill = jnp.zeros((SUBLANES - FFN_TAPS - 1, w_dw.shape[1]), _F32)
    wdw = jnp.concatenate([w_dw, b_dw[None, :], fill], axis=0)
    return w_up.astype(_BF16), wdw, w_down.astype(_BF16)


def kernel(x, fox_norm_g, fox_w_in, fox_b_f, fox_q_g, fox_k_g, fox_w_o, conv_norm_g, conv_w_pw1,
           conv_b_pw1, conv_w_dw, conv_b_dw, conv_ln_g, conv_w_pw2, conv_b_pw2, ffn_norm_g,
           ffn_w_up, ffn_w_dw, ffn_b_dw, ffn_w_down):
    batch, seq, d = x.shape
    depth = ffn_w_up.shape[0]
    x2d = x.reshape(batch * seq, d)
    for layer in range(depth):
        j = layer // 2
        wup, fdw, wdown = _prep_ffn(ffn_w_up[layer], ffn_w_dw[layer], ffn_b_dw[layer],
                                    ffn_w_down[layer])
        fg = ffn_norm_g[layer].reshape(1, d)
        if layer % 2 == 0:
            w, bf, qg, kg, bd = _prep_fox(fox_w_in[j], fox_b_f[j], fox_q_g[j], fox_k_g[j])
            qkv, cum, cumt = _fox_inproj(x2d, fox_norm_g[j].reshape(1, d), w, bf, qg, kg, bd,
                                         seq=seq)
            o = _fox_attn(qkv.reshape(batch, seq, 3 * d), cum.reshape(batch, seq, LANES), cumt)
            attn = (o.reshape(batch * seq, d), fox_w_o[j].astype(_BF16))
            x2d = _ffn(x2d, fg, wup, fdw, wdown, seq=seq, attn=attn)
        else:
            cdw = jnp.concatenate([conv_w_dw[j], conv_b_dw[j][None, :]], axis=0)
            x2d = _conformer_ffn(x2d, conv_norm_g[j].reshape(1, d), conv_w_pw1[j].astype(_BF16),
                                 conv_b_pw1[j].reshape(1, -1), cdw, conv_ln_g[j].reshape(1, -1),
                                 conv_w_pw2[j].astype(_BF16), conv_b_pw2[j].reshape(1, d),
                                 fg, wup, fdw, wdown, seq=seq)
    return x2d.reshape(batch, seq, d)
```

```python
import functools

import jax
import jax.numpy as jnp
from jax import lax
from jax.experimental import pallas as pl
from jax.experimental.pallas import tpu as pltpu

D_MODEL = 1024
HEAD_DIM = 64
N_HEADS = D_MODEL // HEAD_DIM
HEADS_PER_BLOCK = 2
N_HEAD_BLOCKS = N_HEADS // HEADS_PER_BLOCK
FFN_DIM = 2816
FFN_TAPS = 3
CONV_TAPS = 31
EPS = 1e-6

LANES = 128
SUBLANES = 8
MXU_DIM = 256
VMEM_LIMIT_BYTES = 56 * 1024 * 1024

ROW_TILE = 1024
FUSED_TILE = 512
ATTN_TILE = 256
FFN_CHUNK = 256
N_FFN_CHUNKS = FFN_DIM // FFN_CHUNK
CONV_HALO = 32
FFN_HALO = 16
CONV_ROW_BLOCK = 32
FFN_STEPS_PER_CONV_BLOCK = 4
GATE_COLS = LANES
MASK_VALUE = -1e30
LOG2_E = 1.4426950408889634

_F32 = jnp.float32
_BF16 = jnp.bfloat16


def _rms_rows(x, gain):
    ms = jnp.mean(x * x, axis=-1, keepdims=True)
    return x * lax.rsqrt(ms + EPS) * gain


def _sigmoid(x):
    return 1.0 / (1.0 + jnp.exp(-x))


def _log_sigmoid(x):
    return jnp.minimum(x, 0.0) - jnp.log1p(jnp.exp(-jnp.abs(x)))


def _compiler_params(semantics):
    return pltpu.CompilerParams(dimension_semantics=semantics,
                                vmem_limit_bytes=VMEM_LIMIT_BYTES)


def _resident():
    return pl.BlockSpec(memory_space=pltpu.VMEM)


def _fox_inproj_kernel(x_ref, g_ref, w_ref, bf_ref, qg_ref, kg_ref, bd_ref,
                       qkv_ref, cum_ref, cumt_ref, h_ref, carry_ref, *, tiles_per_seq):
    i = pl.program_id(0)
    tm = x_ref.shape[0]
    h_ref[...] = _rms_rows(x_ref[...], g_ref[...]).astype(_BF16)

    n_chunk = 2 * MXU_DIM
    for c in range(3 * D_MODEL // n_chunk):
        cols = slice(c * n_chunk, (c + 1) * n_chunk)
        acc = jnp.dot(h_ref[...], w_ref[:, cols], preferred_element_type=_F32)
        if c * n_chunk < 2 * D_MODEL:
            is_q = c * n_chunk < D_MODEL
            gain_ref = qg_ref if is_q else kg_ref
            gcols = slice((c * n_chunk) % D_MODEL, (c * n_chunk) % D_MODEL + n_chunk)
            sq = (acc * acc).astype(_BF16)
            ssq = jnp.concatenate(
                [jnp.dot(sq[:, s * MXU_DIM:(s + 1) * MXU_DIM], bd_ref[...],
                         preferred_element_type=_F32) for s in range(n_chunk // MXU_DIM)],
                axis=1)
            acc = acc * lax.rsqrt(ssq * (1.0 / HEAD_DIM) + EPS) * gain_ref[:, gcols]
            if is_q:
                acc = acc * (HEAD_DIM ** -0.5 * LOG2_E)
        qkv_ref[:, cols] = acc.astype(_BF16)

    logit = jnp.dot(h_ref[...], w_ref[:, 3 * D_MODEL:3 * D_MODEL + GATE_COLS],
                    preferred_element_type=_F32) + bf_ref[...]
    c = _log_sigmoid(logit)
    row = lax.broadcasted_iota(jnp.int32, c.shape, 0)
    d = 1
    while d < tm:
        c = c + jnp.where(row >= d, pltpu.roll(c, d, axis=0), 0.0)
        d *= 2

    @pl.when(i % tiles_per_seq == 0)
    def _():
        carry_ref[...] = jnp.zeros_like(carry_ref)

    c = c + carry_ref[0:1, :]
    carry_ref[...] = jnp.broadcast_to(c[tm - 1:tm, :], carry_ref.shape)
    c = c * LOG2_E
    cum_ref[...] = c
    ct = c.T
    for hb in range(N_HEAD_BLOCKS):
        cumt_ref[0, hb] = ct[hb * HEADS_PER_BLOCK:(hb + 1) * HEADS_PER_BLOCK, :]


def _fox_inproj(x2d, g, w, bf, qg, kg, bd, *, seq):
    n = x2d.shape[0]
    tm = ROW_TILE
    tps = seq // tm
    batch = n // seq
    return pl.pallas_call(
        functools.partial(_fox_inproj_kernel, tiles_per_seq=tps),
        grid=(n // tm,),
        in_specs=[
            pl.BlockSpec((tm, D_MODEL), lambda i: (i, 0)),
            _resident(), _resident(), _resident(), _resident(), _resident(), _resident(),
        ],
        out_specs=[
            pl.BlockSpec((tm, 3 * D_MODEL), lambda i: (i, 0)),
            pl.BlockSpec((tm, LANES), lambda i: (i, 0)),
            pl.BlockSpec((1, N_HEAD_BLOCKS, HEADS_PER_BLOCK, tm),
                         lambda i: (i // tps, 0, 0, i % tps)),
        ],
        out_shape=[
            jax.ShapeDtypeStruct((n, 3 * D_MODEL), _BF16),
            jax.ShapeDtypeStruct((n, LANES), _F32),
            jax.ShapeDtypeStruct((batch, N_HEAD_BLOCKS, HEADS_PER_BLOCK, seq), _F32),
        ],
        scratch_shapes=[pltpu.VMEM((tm, D_MODEL), _BF16),
                        pltpu.VMEM((SUBLANES, LANES), _F32)],
        compiler_params=_compiler_params(("arbitrary",)),
        name="fox_inproj",
    )(x2d, g, w, bf, qg, kg, bd)


def _fox_attn_kernel(q_ref, k_ref, v_ref, cum_ref, cumt_ref, o_ref, qs_ref, s_ref, p_ref):
    hb = pl.program_id(1)
    seq = q_ref.shape[1]
    tq = ATTN_TILE
    tk = ATTN_TILE
    lane = lax.broadcasted_iota(jnp.int32, (tq, LANES), 1)
    causal = (lax.broadcasted_iota(jnp.int32, (tq, tk), 1)
              <= lax.broadcasted_iota(jnp.int32, (tq, tk), 0))
    heads = range(HEADS_PER_BLOCK)

    def fold(a):
        return [a[:, b * LANES:(b + 1) * LANES] for b in range(tk // LANES)]

    def logits(i):
        par = i % 2
        qrows = slice(i * tq, (i + 1) * tq)
        q = q_ref[0, qrows, :]
        cum = cum_ref[0, qrows, :]
        cq = []
        for e in heads:
            in_head = (lane >= e * HEAD_DIM) & (lane < (e + 1) * HEAD_DIM)
            qs_ref[par, e * tq:(e + 1) * tq, :] = jnp.where(in_head, q, jnp.zeros_like(q))
            head = hb * HEADS_PER_BLOCK + e
            cq.append(jnp.sum(jnp.where(lane == head, cum, 0.0), axis=1, keepdims=True))

        mx = [jnp.full((tq, LANES), MASK_VALUE, _F32) for _ in heads]
        for j in range(i + 1):
            kcols = slice(j * tk, (j + 1) * tk)
            s = lax.dot_general(qs_ref[par], k_ref[0, kcols, :], (((1,), (1,)), ((), ())),
                                preferred_element_type=_F32)
            for e in heads:
                se = s[e * tq:(e + 1) * tq, :] - cumt_ref[0, 0, e:e + 1, kcols]
                if j == i:
                    se = jnp.where(causal, se, MASK_VALUE)
                s_ref[par, e * tq:(e + 1) * tq, kcols] = se
                for part in fold(se):
                    mx[e] = jnp.maximum(mx[e], part)

        shift = []
        for e in heads:
            m = jnp.max(mx[e], axis=1, keepdims=True) + cq[e]
            shift.append(cq[e] - m)
        return shift

    def weighted_values(i, shift):
        par = i % 2
        qrows = slice(i * tq, (i + 1) * tq)
        ls = [jnp.zeros((tq, LANES), _F32) for _ in heads]
        for j in range(i + 1):
            kcols = slice(j * tk, (j + 1) * tk)
            for e in heads:
                erows = slice(e * tq, (e + 1) * tq)
                p = jnp.exp2(s_ref[par, erows, kcols] + shift[e])
                for part in fold(p):
                    ls[e] = ls[e] + part
                p_ref[par, erows, kcols] = p.astype(_BF16)

        q1 = (i + 1) * tq
        pv = jnp.dot(p_ref[par, :, :q1], v_ref[0, :q1, :], preferred_element_type=_F32)
        outs = [pv[e * tq:(e + 1) * tq, :] / jnp.sum(ls[e], axis=1, keepdims=True) for e in heads]
        o_ref[0, qrows, :] = jnp.where(lane < HEAD_DIM, outs[0], outs[1]).astype(o_ref.dtype)

    order = list(reversed(range(seq // tq)))
    shift = logits(order[0])
    for n, i in enumerate(order):
        upcoming = logits(order[n + 1]) if n + 1 < len(order) else None
        weighted_values(i, shift)
        shift = upcoming


def _fox_attn(qkv, cum, cumt):
    batch, seq, _ = qkv.shape
    rows = HEADS_PER_BLOCK * ATTN_TILE
    return pl.pallas_call(
        _fox_attn_kernel,
        grid=(batch, N_HEAD_BLOCKS),
        in_specs=[
            pl.BlockSpec((1, seq, LANES), lambda b, h: (b, 0, h)),
            pl.BlockSpec((1, seq, LANES), lambda b, h: (b, 0, N_HEAD_BLOCKS + h)),
            pl.BlockSpec((1, seq, LANES), lambda b, h: (b, 0, 2 * N_HEAD_BLOCKS + h)),
            pl.BlockSpec((1, seq, LANES), lambda b, h: (b, 0, 0)),
            pl.BlockSpec((1, 1, HEADS_PER_BLOCK, seq), lambda b, h: (b, h, 0, 0)),
        ],
        out_specs=pl.BlockSpec((1, seq, LANES), lambda b, h: (b, 0, h)),
        out_shape=jax.ShapeDtypeStruct((batch, seq, D_MODEL), _BF16),
        scratch_shapes=[pltpu.VMEM((2, rows, LANES), _BF16),
                        pltpu.VMEM((2, rows, seq), _F32),
                        pltpu.VMEM((2, rows, seq), _BF16)],
        compiler_params=_compiler_params(("arbitrary", "arbitrary")),
        name="fox_attn",
    )(qkv, qkv, qkv, cum, cumt)


def _ffn_steps(x, xprev, first, g_ref, wup_ref, wdw_ref, wdown_ref, y_ref, h_ref, u_ref, a_ref):
    tm = y_ref.shape[0]
    halo = FFN_HALO
    n_slab = FFN_CHUNK // LANES

    def prologue():
        xv = x()
        y_ref[...] = xv
        h_ref[halo:, :] = _rms_rows(xv, g_ref[...]).astype(_BF16)
        hprev = _rms_rows(xprev(), g_ref[...])
        h_ref[:halo, :] = jnp.where(first, 0.0, hprev).astype(_BF16)

    def cols(c, half, j=0, width=FFN_CHUNK):
        lo = half * FFN_DIM + c * FFN_CHUNK + j * LANES
        return slice(lo, lo + width)

    def up(c, slot):
        for half in range(2):
            u = jnp.dot(h_ref[...], wup_ref[:, cols(c, half)], preferred_element_type=_F32)
            for j in range(n_slab):
                u_ref[slot, half * n_slab + j] = u[:, j * LANES:(j + 1) * LANES]

    def conv(c, slot, half, j):
        ln = cols(c, half, j, LANES)
        y = wdw_ref[FFN_TAPS:FFN_TAPS + 1, ln]
        for k in range(FFN_TAPS):
            off = halo - (FFN_TAPS - 1) + k
            y = y + wdw_ref[k:k + 1, ln] * u_ref[slot, half * n_slab + j, off:off + tm, :]
        return y

    def act(c, slot):
        parts = []
        for j in range(n_slab):
            gate = conv(c, slot, 0, j)
            val = conv(c, slot, 1, j)
            parts.append((gate * _sigmoid(gate)) * val)
        a_ref[slot] = jnp.concatenate(parts, axis=1).astype(_BF16)

    def down(c, slot):
        rows = slice(c * FFN_CHUNK, (c + 1) * FFN_CHUNK)
        y_ref[...] += jnp.dot(a_ref[slot], wdown_ref[rows, :], preferred_element_type=_F32)

    def first_step():
        prologue()
        up(0, 0)

    def second_step():
        up(1, 1)
        act(0, 0)

    def middle_step(c):
        def run():
            slot = c % 2
            up(c + 1, 1 - slot)
            down(c - 1, 1 - slot)
            act(c, slot)
        return run

    last = N_FFN_CHUNKS - 1

    def last_step():
        down(last - 1, (last - 1) % 2)
        act(last, last % 2)
        down(last, last % 2)

    return [first_step, second_step] + [middle_step(c) for c in range(1, last)] + [last_step]


def _ffn_kernel(*refs, tiles_per_seq, with_attn):
    if with_attn:
        (x_ref, xprev_ref, o_ref, oprev_ref, wo_ref, g_ref, wup_ref, wdw_ref, wdown_ref, y_ref,
         h_ref, u_ref, a_ref, oext_ref) = refs
    else:
        (x_ref, xprev_ref, g_ref, wup_ref, wdw_ref, wdown_ref, y_ref,
         h_ref, u_ref, a_ref) = refs
    i = pl.program_id(0)
    halo = FFN_HALO
    if with_attn:
        oext_ref[:halo, :] = oprev_ref[...]
        oext_ref[halo:, :] = o_ref[...]
        upd = jnp.dot(oext_ref[...], wo_ref[...], preferred_element_type=_F32)
        x = lambda: x_ref[...] + upd[halo:, :]
        xprev = lambda: xprev_ref[...] + upd[:halo, :]
    else:
        x = lambda: x_ref[...]
        xprev = lambda: xprev_ref[...]
    for step in _ffn_steps(x, xprev, i % tiles_per_seq == 0, g_ref, wup_ref, wdw_ref, wdown_ref,
                           y_ref, h_ref, u_ref, a_ref):
        step()


def _ffn(x2d, g, wup, wdw, wdown, *, seq, attn=None):
    n = x2d.shape[0]
    tm = ROW_TILE
    tps = seq // tm
    hblocks = tm // FFN_HALO
    row_spec = pl.BlockSpec((tm, D_MODEL), lambda i: (i, 0))
    halo_spec = pl.BlockSpec((FFN_HALO, D_MODEL), lambda i: (jnp.maximum(i * hblocks - 1, 0), 0))
    weights = [_resident()] * 4
    scratch = [pltpu.VMEM((tm + FFN_HALO, D_MODEL), _BF16),
               pltpu.VMEM((2, 2 * FFN_CHUNK // LANES, tm + FFN_HALO, LANES), _F32),
               pltpu.VMEM((2, tm, FFN_CHUNK), _BF16)]
    if attn is None:
        in_specs = [row_spec, halo_spec] + weights
        args = (x2d, x2d, g, wup, wdw, wdown)
    else:
        o2d, wo = attn
        in_specs = [row_spec, halo_spec, row_spec, halo_spec, _resident()] + weights
        args = (x2d, x2d, o2d, o2d, wo, g, wup, wdw, wdown)
        scratch = scratch + [pltpu.VMEM((tm + FFN_HALO, D_MODEL), _BF16)]
    return pl.pallas_call(
        functools.partial(_ffn_kernel, tiles_per_seq=tps, with_attn=attn is not None),
        grid=(n // tm,),
        in_specs=in_specs,
        out_specs=row_spec,
        out_shape=jax.ShapeDtypeStruct((n, D_MODEL), _F32),
        scratch_shapes=scratch,
        compiler_params=_compiler_params(("arbitrary",)),
        name="conv_ffn",
    )(*args)


def _conformer_steps(x_ref, xprev_ref, first, g_ref, w1_ref, b1_ref, wdw_ref, lng_ref, w2_ref, b2_ref,
                     out_ref, h_ref, glu_ref, conv_ref, act_ref):
    tm = x_ref.shape[0]
    halo = CONV_HALO
    n_slab = glu_ref.shape[0]
    c_inner = n_slab * LANES
    slabs_per_block = MXU_DIM // LANES
    n_blocks = n_slab // slabs_per_block
    rb = CONV_ROW_BLOCK
    tap0 = halo - (CONV_TAPS - 1)

    def norm():
        h_ref[halo:, :] = _rms_rows(x_ref[...], g_ref[...]).astype(_BF16)
        h_ref[:halo, :] = _rms_rows(xprev_ref[...], g_ref[...]).astype(_BF16)

    def glu_block(cb):
        row = lax.broadcasted_iota(jnp.int32, (tm + halo, MXU_DIM), 0)
        keep = jnp.logical_or(row >= halo, jnp.logical_not(first))
        ca = slice(cb * MXU_DIM, (cb + 1) * MXU_DIM)
        cg = slice(c_inner + cb * MXU_DIM, c_inner + (cb + 1) * MXU_DIM)
        a = jnp.dot(h_ref[...], w1_ref[:, ca], preferred_element_type=_F32) + b1_ref[:, ca]
        g = jnp.dot(h_ref[...], w1_ref[:, cg], preferred_element_type=_F32) + b1_ref[:, cg]
        glu = jnp.where(keep, a * _sigmoid(g), 0.0)
        for j in range(slabs_per_block):
            glu_ref[cb * slabs_per_block + j] = glu[:, j * LANES:(j + 1) * LANES]

    def conv_slab(lt):
        ln = slice(lt * LANES, (lt + 1) * LANES)
        for base in range(0, tm, rb):
            acc = jnp.broadcast_to(wdw_ref[CONV_TAPS:CONV_TAPS + 1, ln], (rb, LANES))
            for k in range(CONV_TAPS):
                lo = base + tap0 + k
                acc = acc + wdw_ref[k:k + 1, ln] * glu_ref[lt, lo:lo + rb, :]
            conv_ref[lt, base:base + rb, :] = acc

    def tail():
        ssq = conv_ref[0] * conv_ref[0]
        for lt in range(1, n_slab):
            ssq = ssq + conv_ref[lt] * conv_ref[lt]
        inv = lax.rsqrt(jnp.sum(ssq, axis=1, keepdims=True) * (1.0 / c_inner) + EPS)
        parts = []
        for lt in range(n_slab):
            ln = slice(lt * LANES, (lt + 1) * LANES)
            u = conv_ref[lt] * inv * lng_ref[:, ln]
            parts.append(u * _sigmoid(u))
        act_ref[...] = jnp.concatenate(parts, axis=1).astype(_BF16)
        out_ref[...] = (x_ref[...] + b2_ref[...]
                        + jnp.dot(act_ref[...], w2_ref[...], preferred_element_type=_F32))

    def block_step(cb):
        def run():
            if cb == 0:
                norm()
            if cb < n_blocks:
                glu_block(cb)
            if cb >= 1:
                for j in range(slabs_per_block):
                    conv_slab((cb - 1) * slabs_per_block + j)
        return run

    return [block_step(cb) for cb in range(n_blocks + 1)] + [tail]


def _conformer_ffn_kernel(x_ref, xprev_ref, cg_ref, w1_ref, b1_ref, cdw_ref, lng_ref, w2_ref, b2_ref,
                          fg_ref, wup_ref, fdw_ref, wdown_ref, y_ref,
                          ch_ref, glu_ref, conv_ref, act_ref, mid_ref, midprev_ref,
                          fh_ref, u_ref, a_ref, *, tiles_per_seq):
    i = pl.program_id(0)

    @pl.when(i == 0)
    def _():
        mid_ref[...] = jnp.zeros_like(mid_ref)
        midprev_ref[...] = jnp.zeros_like(midprev_ref)

    mid_prev = midprev_ref[...]
    ffn_first = (i + tiles_per_seq - 1) % tiles_per_seq == 0
    ffn = _ffn_steps(lambda: mid_ref[...], lambda: mid_prev, ffn_first, fg_ref, wup_ref, fdw_ref,
                     wdown_ref, y_ref, fh_ref, u_ref, a_ref)
    conf = _conformer_steps(x_ref, xprev_ref, i % tiles_per_seq == 0, cg_ref, w1_ref, b1_ref, cdw_ref,
                            lng_ref, w2_ref, b2_ref, mid_ref, ch_ref, glu_ref, conv_ref, act_ref)
    ffn[0]()
    midprev_ref[...] = mid_ref[mid_ref.shape[0] - FFN_HALO:, :]
    per = FFN_STEPS_PER_CONV_BLOCK
    fi = 1
    for cstep in conf[:-1]:
        cstep()
        for fstep in ffn[fi:fi + per]:
            fstep()
        fi += per
    for fstep in ffn[fi:]:
        fstep()
    conf[-1]()


def _conformer_ffn(x2d, cg, w1, b1, cdw, lng, w2, b2, fg, wup, fdw, wdown, *, seq):
    n = x2d.shape[0]
    tm = FUSED_TILE
    tps = seq // tm
    n_tiles = n // tm
    hblocks = tm // CONV_HALO
    c_inner = w2.shape[0]
    last = n_tiles - 1
    return pl.pallas_call(
        functools.partial(_conformer_ffn_kernel, tiles_per_seq=tps),
        grid=(n_tiles + 1,),
        in_specs=[
            pl.BlockSpec((tm, D_MODEL), lambda i: (jnp.minimum(i, last), 0)),
            pl.BlockSpec((CONV_HALO, D_MODEL),
                         lambda i: (jnp.maximum(jnp.minimum(i, last) * hblocks - 1, 0), 0)),
        ] + [_resident()] * 11,
        out_specs=pl.BlockSpec((tm, D_MODEL), lambda i: (jnp.maximum(i - 1, 0), 0)),
        out_shape=jax.ShapeDtypeStruct((n, D_MODEL), _F32),
        scratch_shapes=[pltpu.VMEM((tm + CONV_HALO, D_MODEL), _BF16),
                        pltpu.VMEM((c_inner // LANES, tm + CONV_HALO, LANES), _F32),
                        pltpu.VMEM((c_inner // LANES, tm, LANES), _F32),
                        pltpu.VMEM((tm, c_inner), _BF16),
                        pltpu.VMEM((tm, D_MODEL), _F32),
                        pltpu.VMEM((FFN_HALO, D_MODEL), _F32),
                        pltpu.VMEM((tm + FFN_HALO, D_MODEL), _BF16),
                        pltpu.VMEM((2, 2 * FFN_CHUNK // LANES, tm + FFN_HALO, LANES), _F32),
                        pltpu.VMEM((2, tm, FFN_CHUNK), _BF16)],
        compiler_params=_compiler_params(("arbitrary",)),
        name="conformer_ffn",
    )(x2d, x2d, cg, w1, b1, cdw, lng, w2, b2, fg, wup, fdw, wdown)


def _prep_fox(w_in, b_f, q_g, k_g):
    pad = GATE_COLS - N_HEADS
    w = jnp.pad(w_in, ((0, 0), (0, pad))).astype(_BF16)
    bf = jnp.pad(b_f, (0, pad)).reshape(1, GATE_COLS)
    head = jnp.arange(MXU_DIM) // HEAD_DIM
    bd = (head[:, None] == head[None, :]).astype(_BF16)
    return w, bf, q_g.reshape(1, D_MODEL), k_g.reshape(1, D_MODEL), bd


def _prep_ffn(w_up, w_dw, b_dw, w_down):
    fill = jnp.zeros((SUBLANES - FFN_TAPS - 1, w_dw.shape[1]), _F32)
    wdw = jnp.concatenate([w_dw, b_dw[None, :], fill], axis=0)
    return w_up.astype(_BF16), wdw, w_down.astype(_BF16)


def kernel(x, fox_norm_g, fox_w_in, fox_b_f, fox_q_g, fox_k_g, fox_w_o, conv_norm_g, conv_w_pw1,
           conv_b_pw1, conv_w_dw, conv_b_dw, conv_ln_g, conv_w_pw2, conv_b_pw2, ffn_norm_g,
           ffn_w_up, ffn_w_dw, ffn_b_dw, ffn_w_down):
    batch, seq, d = x.shape
    depth = ffn_w_up.shape[0]
    x2d = x.reshape(batch * seq, d)
    for layer in range(depth):
        j = layer // 2
        wup, fdw, wdown = _prep_ffn(ffn_w_up[layer], ffn_w_dw[layer], ffn_b_dw[layer],
                                    ffn_w_down[layer])
        fg = ffn_norm_g[layer].reshape(1, d)
        if layer % 2 == 0:
            w, bf, qg, kg, bd = _prep_fox(fox_w_in[j], fox_b_f[j], fox_q_g[j], fox_k_g[j])
            qkv, cum, cumt = _fox_inproj(x2d, fox_norm_g[j].reshape(1, d), w, bf, qg, kg, bd,
                                         seq=seq)
            o = _fox_attn(qkv.reshape(batch, seq, 3 * d), cum.reshape(batch, seq, LANES), cumt)
            attn = (o.reshape(batch * seq, d), fox_w_o[j].astype(_BF16))
            x2d = _ffn(x2d, fg, wup, fdw, wdown, seq=seq, attn=attn)
        else:
            cdw = jnp.concatenate([conv_w_dw[j], conv_b_dw[j][None, :]], axis=0)
            x2d = _conformer_ffn(x2d, conv_norm_g[j].reshape(1, d), conv_w_pw1[j].astype(_BF16),
                                 conv_b_pw1[j].reshape(1, -1), cdw, conv_ln_g[j].reshape(1, -1),
                                 conv_w_pw2[j].astype(_BF16), conv_b_pw2[j].reshape(1, d),
                                 fg, wup, fdw, wdown, seq=seq)
    return x2d.reshape(batch, seq, d)
```
